```python
import functools
import jax
import jax.numpy as jnp
from jax import lax
import numpy as np

D_MODEL = 1024
BATCH = 4
SEQ = 4096
DEPTH = 4

GRID_W = 64
CTX_LEN = 256
EPS = 1e-6

HG_HEADS = 4
HG_DK = 128
HG_DV = 128
HG_CHUNK = 64
HG_W = HG_HEADS * HG_DK
HG_OUT = HG_HEADS * HG_DV

MLA_HEADS = 4
MLA_Q_RANK = 256
MLA_KV_RANK = 128
MLA_NOPE = 128
MLA_ROPE = 64
MLA_V = 128
MLA_OUT = MLA_HEADS * MLA_V
MLA_SCALE = (MLA_NOPE + MLA_ROPE) ** -0.5
ROPE_THETA = 10000.0
Q_BLOCK = 128

IN_SIZES = (HG_W, HG_W, HG_W, HG_OUT, HG_OUT, MLA_Q_RANK, MLA_KV_RANK, MLA_ROPE)
IN_WIDTH = 3008
MIX_OUT = HG_OUT + MLA_OUT

POOL_WINDOWS = (2, 4, 8, 16)
POOL_GROUP = D_MODEL // 4

N_EXPERTS = 16
EC_FACTOR = 2
EXPERT_FF = 2048

N_EVEN = (DEPTH + 1) // 2
N_ODD = DEPTH // 2

kernel_name = 'hybrid_flow_trunk'


def rms_norm(x, g):
    xf = x.astype(jnp.float32)
    y = xf * lax.rsqrt(jnp.mean(xf * xf, axis=-1, keepdims=True) + EPS)
    return (y * g.astype(jnp.float32)).astype(x.dtype)


def modulate(x, g, shift, scale):
    return rms_norm(x, g) * (1 + scale) + shift


def to_heads(t, n_heads):
    b, n, _ = t.shape
    return t.reshape(b, n, n_heads, -1).transpose(0, 2, 1, 3)


def from_heads(t):
    b, h, n, d = t.shape
    return t.transpose(0, 2, 1, 3).reshape(b, n, h * d)


def hg_heads(t):
    return to_heads(t, HG_HEADS).astype(jnp.float32)


def maybe_flip(t, rev):
    return jnp.flip(t, axis=2) if rev else t


def axial_rope(n_tok):
    rows = n_tok // GRID_W
    row = jnp.repeat(jnp.arange(rows, dtype=jnp.float32), GRID_W)
    col = jnp.tile(jnp.arange(GRID_W, dtype=jnp.float32), rows)
    half = MLA_ROPE // 2
    inv = 1.0 / (ROPE_THETA ** (jnp.arange(0, half, 2, dtype=jnp.float32) / half))
    ar = row[:, None] * inv[None, :]
    ac = col[:, None] * inv[None, :]
    ang = jnp.concatenate([ar, ar, ac, ac], axis=-1)
    return jnp.cos(ang), jnp.sin(ang)


def apply_rope(x, cos, sin):
    x0, x1, x2, x3 = jnp.split(x, 4, axis=-1)
    rot = jnp.concatenate([-x1, x0, -x3, x2], axis=-1)
    return (x.astype(jnp.float32) * cos + rot.astype(jnp.float32) * sin).astype(x.dtype)


def hgrn_gates(z, lb):
    lb = lb[None, :, None, :]
    logf = jnp.logaddexp(jnp.log(lb), jnp.log1p(-lb) + jax.nn.log_sigmoid(z))
    k = (1.0 - lb) * jax.nn.sigmoid(-z)
    return logf, k


def chunk_scan(q, k, v, logf, s0):
    b, h, n, _ = q.shape
    dv = v.shape[-1]
    nc = n // HG_CHUNK

    def to_chunks(t):
        return jnp.moveaxis(t.reshape(b, h, nc, HG_CHUNK, t.shape[-1]), 2, 0)

    tri = jnp.tril(jnp.ones((HG_CHUNK, HG_CHUNK), dtype=bool))[:, :, None]

    def step(s, inp):
        qc, kc, vc, gc = inp
        cb = jnp.cumsum(gc, axis=2)
        o_inter = jnp.einsum('bhtk,bhkv->bhtv', qc * jnp.exp(cb), s)
        diff = cb[:, :, :, None, :] - cb[:, :, None, :, :]
        dec = jnp.exp(jnp.where(tri, diff, -jnp.inf))
        att = jnp.einsum('bhtk,bhsk,bhtsk->bhts', qc, kc, dec)
        o = o_inter + jnp.einsum('bhts,bhsv->bhtv', att, vc)
        cb_last = cb[:, :, -1:, :]
        s_new = jnp.exp(cb_last[:, :, 0, :, None]) * s + jnp.einsum('bhsk,bhsv->bhkv', kc * jnp.exp(cb_last - cb), vc)
        return s_new, o

    s_fin, o = lax.scan(step, s0, (to_chunks(q), to_chunks(k), to_chunks(v), to_chunks(logf)))
    return jnp.moveaxis(o, 0, 2).reshape(b, h, n, dv), s_fin


def final_state(k, v, logf):
    g = jnp.cumsum(logf, axis=2)
    w = jnp.exp(g[:, :, -1:, :] - g)
    return jnp.einsum('bhnk,bhnv->bhkv', k * w, v)


def hgrn_readout(o, g, hg_g, dtype):
    gate = jax.nn.silu(to_heads(g, HG_HEADS).astype(jnp.float32))
    return from_heads(rms_norm(o, hg_g) * gate).astype(dtype)


def mla_q(qa, qn_g, wq_b):
    b, n, _ = qa.shape
    q = (rms_norm(qa, qn_g) @ wq_b).reshape(b, n, MLA_HEADS, MLA_NOPE + MLA_ROPE)
    return q[..., :MLA_NOPE], q[..., MLA_NOPE:]


def mla_kv(kva, kvn_g, wkv_b):
    b, n, _ = kva.shape
    kv = (rms_norm(kva, kvn_g) @ wkv_b).reshape(b, n, MLA_HEADS, MLA_NOPE + MLA_V)
    return kv[..., :MLA_NOPE], kv[..., MLA_NOPE:]


def block_attention(q_nope, q_pe, k_nope, k_pe, v):
    b, n, h, _ = q_nope.shape
    nb = n // Q_BLOCK

    def blocks(t):
        return jnp.moveaxis(t.reshape(b, nb, Q_BLOCK, h, t.shape[-1]), 1, 0)

    def one_block(args):
        qn, qp = args
        s = jnp.einsum('bqhd,bkhd->bhqk', qn, k_nope) + jnp.einsum('bqhr,bkr->bhqk', qp, k_pe)
        p = jax.nn.softmax(s.astype(jnp.float32) * MLA_SCALE, axis=-1)
        return jnp.einsum('bhqk,bkhd->bqhd', p.astype(v.dtype), v)

    o = lax.map(one_block, (blocks(q_nope), blocks(q_pe)))
    return jnp.moveaxis(o, 0, 1).reshape(b, n, h * v.shape[-1])


def even_mixer(h_lat, h_ctx, w_in, lb, hg_g, qn_g, wq_b, kvn_g, wkv_b, w_out, cos, sin, ctx_out):
    offs = np.cumsum(IN_SIZES)[:-1].tolist()
    q_l, ffw_l, fbw_l, i_l, g_l, qa_l, kva_l, kpe_l = jnp.split(h_lat @ w_in, offs, axis=-1)
    q_c, ffw_c, fbw_c, i_c, g_c, qa_c, kva_c, kpe_c = jnp.split(h_ctx @ w_in, offs, axis=-1)
    ql, vl, vc = hg_heads(q_l), hg_heads(i_l), hg_heads(i_c)
    qc = hg_heads(q_c) if ctx_out else None
    s0 = jnp.zeros((h_lat.shape[0], HG_HEADS, HG_DK, HG_DV), jnp.float32)
    lat_o, ctx_o = [], []
    for d, (z_l, z_c) in enumerate(((ffw_l, ffw_c), (fbw_l, fbw_c))):
        rev = d == 1
        lb_d = lb[d].reshape(HG_HEADS, HG_DK)
        logf_l, k_l = hgrn_gates(hg_heads(z_l), lb_d)
        logf_c, k_c = hgrn_gates(hg_heads(z_c), lb_d)
        if ctx_out:
            o_c, s_c = chunk_scan(maybe_flip(qc, rev), maybe_flip(k_c, rev), maybe_flip(vc, rev), maybe_flip(logf_c, rev), s0)
            ctx_o.append(maybe_flip(o_c, rev))
        else:
            s_c = final_state(maybe_flip(k_c, rev), maybe_flip(vc, rev), maybe_flip(logf_c, rev))
        o_l, _ = chunk_scan(maybe_flip(ql, rev), maybe_flip(k_l, rev), maybe_flip(vl, rev), maybe_flip(logf_l, rev), s_c)
        lat_o.append(maybe_flip(o_l, rev))
    hg_lat = hgrn_readout(lat_o[0] + lat_o[1], g_l, hg_g, h_lat.dtype)
    kn_l, v_l = mla_kv(kva_l, kvn_g, wkv_b)
    kn_c, v_c = mla_kv(kva_c, kvn_g, wkv_b)
    qn_l, qp_l = mla_q(qa_l, qn_g, wq_b)
    qp_l = apply_rope(qp_l, cos[:, None, :], sin[:, None, :])
    kp_l = apply_rope(kpe_l, cos, sin)
    mla_lat = block_attention(qn_l, qp_l,
                              jnp.concatenate([kn_c, kn_l], axis=1),
                              jnp.concatenate([kpe_c, kp_l], axis=1),
                              jnp.concatenate([v_c, v_l], axis=1))
    y_lat = jnp.concatenate([hg_lat, mla_lat], axis=-1) @ w_out
    if not ctx_out:
        return y_lat, None
    hg_ctx = hgrn_readout(ctx_o[0] + ctx_o[1], g_c, hg_g, h_ctx.dtype)
    qn_c, qp_c = mla_q(qa_c, qn_g, wq_b)
    mla_ctx = block_attention(qn_c, qp_c, kn_c, kpe_c, v_c)
    y_ctx = jnp.concatenate([hg_ctx, mla_ctx], axis=-1) @ w_out
    return y_lat, y_ctx


def pool_mixer(h, w_pool, scale):
    b, n, d = h.shape
    hg = h.astype(jnp.float32).reshape(b, n, len(POOL_WINDOWS), POOL_GROUP)
    cs = jnp.concatenate([jnp.zeros((b, 1, len(POOL_WINDOWS), POOL_GROUP), jnp.float32), jnp.cumsum(hg, axis=1)], axis=1)
    t = jnp.arange(n)
    pooled = []
    for gi, w in enumerate(POOL_WINDOWS):
        lo = jnp.clip(t - w // 2, 0, n - 1)
        hi = jnp.clip(t + w // 2 - 1, 0, n - 1)
        csg = cs[:, :, gi]
        mean = (csg[:, hi + 1] - csg[:, lo]) / (hi - lo + 1).astype(jnp.float32)[None, :, None]
        pooled.append(mean - hg[:, :, gi])
    y = jnp.einsum('bngc,gcd->bngd', jnp.stack(pooled, axis=2), w_pool.astype(jnp.float32))
    return (y.reshape(b, n, d) * scale.astype(jnp.float32)).astype(h.dtype)


def ec_moe(h, router_w, wg, wu, wd):
    b, n, d = h.shape
    cap = EC_FACTOR * n // N_EXPERTS
    aff = jax.nn.softmax(jnp.einsum('bnd,de->bne', h, router_w).astype(jnp.float32), axis=-1)
    gate, idx = lax.top_k(jnp.swapaxes(aff, 1, 2), cap)
    xs = jax.vmap(lambda hb, ib: hb[ib])(h, idx)
    hid = jax.nn.silu(jnp.einsum('becd,edf->becf', xs, wg)) * jnp.einsum('becd,edf->becf', xs, wu)
    out = jnp.einsum('becf,efd->becd', hid, wd) * gate[..., None].astype(h.dtype)
    return jax.vmap(lambda ob, ib: jax.ops.segment_sum(ob.reshape(-1, d), ib.reshape(-1), num_segments=n))(out, idx)


def setup_inputs(seed: int = 0) -> dict:
    key = jax.random.key(seed)
    ks = jax.random.split(key, 23)
    D = D_MODEL

    def nrm(k, shape, s):
        return jax.random.normal(k, shape, jnp.float32) * s

    return {
        'x': nrm(ks[0], (BATCH, SEQ, D), 1.0),
        'c': nrm(ks[1], (BATCH, D), 1.0),
        'ctx': nrm(ks[2], (BATCH, CTX_LEN, D), 1.0),
        'c_ctx': nrm(ks[3], (D,), 1.0),
        'ada_w': nrm(ks[4], (DEPTH, D, 6 * D), 0.5 * D ** -0.5),
        'ada_b': nrm(ks[5], (DEPTH, 6 * D), 0.02),
        'norm1_g': 1.0 + nrm(ks[6], (DEPTH, D), 0.05),
        'norm2_g': 1.0 + nrm(ks[7], (DEPTH, D), 0.05),
        'w_in': nrm(ks[8], (N_EVEN, D, IN_WIDTH), D ** -0.5),
        'hg_lb': nrm(ks[9], (N_EVEN, 2, HG_W), 0.1),
        'hg_norm_g': 1.0 + nrm(ks[10], (N_EVEN, HG_DV), 0.05),
        'mla_qn_g': 1.0 + nrm(ks[11], (N_EVEN, MLA_Q_RANK), 0.05),
        'mla_wq_b': nrm(ks[12], (N_EVEN, MLA_Q_RANK, MLA_HEADS * (MLA_NOPE + MLA_ROPE)), MLA_Q_RANK ** -0.5),
        'mla_kvn_g': 1.0 + nrm(ks[13], (N_EVEN, MLA_KV_RANK), 0.05),
        'mla_wkv_b': nrm(ks[14], (N_EVEN, MLA_KV_RANK, MLA_HEADS * (MLA_NOPE + MLA_V)), MLA_KV_RANK ** -0.5),
        'w_out': nrm(ks[15], (N_EVEN, MIX_OUT, D), MIX_OUT ** -0.5),
        'pool_w': nrm(ks[16], (N_ODD, len(POOL_WINDOWS), POOL_GROUP, POOL_GROUP), POOL_GROUP ** -0.5),
        'pool_scale': 1.0 + nrm(ks[17], (N_ODD, D), 0.1),
        'router_w': nrm(ks[18], (DEPTH, D, N_EXPERTS), D ** -0.5),
        'exp_wg': nrm(ks[19], (DEPTH, N_EXPERTS, D, EXPERT_FF), D ** -0.5),
        'exp_wu': nrm(ks[20], (DEPTH, N_EXPERTS, D, EXPERT_FF), D ** -0.5),
        'exp_wd': nrm(ks[21], (DEPTH, N_EXPERTS, EXPERT_FF, D), EXPERT_FF ** -0.5),
        'final_g': 1.0 + nrm(ks[22], (D,), 0.05),
    }


def reference(x, c, ctx, c_ctx, ada_w, ada_b, norm1_g, norm2_g, w_in, hg_lb, hg_norm_g,
              mla_qn_g, mla_wq_b, mla_kvn_g, mla_wkv_b, w_out, pool_w, pool_scale,
              router_w, exp_wg, exp_wu, exp_wd, final_g):
    cos, sin = axial_rope(x.shape[1])
    lb_all = jnp.cumsum(jax.nn.softmax(hg_lb.astype(jnp.float32), axis=0), axis=0)
    lb_all = lb_all - lb_all[:1]
    last_reader = 2 * (N_EVEN - 1)
    x_lat, x_ctx = x, ctx
    for l in range(DEPTH):
        j = l // 2
        ctx_in = l <= last_reader
        ctx_out = l < last_reader
        mod = [m[:, None, :] for m in jnp.split(jax.nn.silu(c) @ ada_w[l] + ada_b[l], 6, axis=-1)]
        h_lat = modulate(x_lat, norm1_g[l], mod[0], mod[1])
        if ctx_in:
            mod_c = jnp.split(jax.nn.silu(c_ctx) @ ada_w[l] + ada_b[l], 6, axis=-1)
            h_ctx = modulate(x_ctx, norm1_g[l], mod_c[0], mod_c[1])
        if l % 2 == 0:
            y_lat, y_ctx = even_mixer(h_lat, h_ctx, w_in[j], lb_all[j], hg_norm_g[j], mla_qn_g[j], mla_wq_b[j],
                                      mla_kvn_g[j], mla_wkv_b[j], w_out[j], cos, sin, ctx_out)
        else:
            y_lat = pool_mixer(h_lat, pool_w[j], pool_scale[j])
            y_ctx = pool_mixer(h_ctx, pool_w[j], pool_scale[j]) if ctx_out else None
        x_lat = x_lat + mod[2] * y_lat
        h2 = modulate(x_lat, norm2_g[l], mod[3], mod[4])
        x_lat = x_lat + mod[5] * ec_moe(h2, router_w[l], exp_wg[l], exp_wu[l], exp_wd[l])
        if ctx_out:
            x_ctx = x_ctx + mod_c[2] * y_ctx
            h2c = modulate(x_ctx, norm2_g[l], mod_c[3], mod_c[4])
            x_ctx = x_ctx + mod_c[5] * ec_moe(h2c, router_w[l], exp_wg[l], exp_wu[l], exp_wd[l])
    return rms_norm(x_lat, final_g)
```

```python
import functools

import numpy as np
import jax
import jax.numpy as jnp
from jax import lax
from jax.experimental import pallas as pl
from jax.experimental.pallas import tpu as pltpu

F32 = jnp.float32
BF16 = jnp.bfloat16
I32 = jnp.int32

EPS = 1e-6
GRID_W = 64
ROPE_THETA = 10000.0
HG_HEADS = 4
HG_D = 128
MLA_HEADS = 4
MLA_Q_RANK = 256
MLA_KV_RANK = 128
MLA_NOPE = 128
MLA_ROPE = 64
MLA_V = 128
MLA_SCALE = (MLA_NOPE + MLA_ROPE) ** -0.5
POOL_WINDOWS = (2, 4, 8, 16)
EC_FACTOR = 2
HG_W = HG_HEADS * HG_D

LANES = 128
CHUNK = 64
POOL_HALO = 8
SUBLANES = 8
BF16_ROWS = 16
GATHER_WIN = LANES + SUBLANES
GATHER_FAST_WIN = 40
SCATTER_WIN = 256
ATTN_KEY_CHUNK = 512
LOG2E = 1.4426950408889634
VMEM_LIMIT_V7X = 56 * 1024 * 1024


def _cparams(*sem):
    return pltpu.CompilerParams(dimension_semantics=sem, vmem_limit_bytes=VMEM_LIMIT_V7X)


def _nt(a, b):
    return lax.dot_general(a, b, (((1,), (1,)), ((), ())), preferred_element_type=F32)


def _tn(a, b):
    return lax.dot_general(a, b, (((0,), (0,)), ((), ())), preferred_element_type=F32)


def _dot(a, b):
    return jnp.dot(a, b, preferred_element_type=F32)


def _sigmoid(x):
    return 1.0 / (1.0 + jnp.exp(-x))


def _rms(x, g):
    return x * lax.rsqrt(jnp.mean(x * x, axis=-1, keepdims=True) + EPS) * g


def _modulate(x, g, shift, scale):
    return _rms(x, g) * (1.0 + scale) + shift


def _split3(x):
    hi = x.astype(BF16)
    r1 = x - hi.astype(F32)
    mid = r1.astype(BF16)
    lo = (r1 - mid.astype(F32)).astype(BF16)
    return hi, mid, lo


def _ada_kernel(c_ref, w_ref, b_ref, o_ref):
    c = c_ref[...]
    s = (c * _sigmoid(c)).astype(BF16)
    o_ref[0] = _dot(s, w_ref[0].astype(BF16)) + b_ref[0]


def _ada(cc, ada_w, ada_b):
    depth, d, d6 = ada_w.shape
    tn = d6 // 4
    return pl.pallas_call(
        _ada_kernel,
        grid=(depth, d6 // tn),
        in_specs=[pl.BlockSpec((8, d), lambda l, j: (0, 0)),
                  pl.BlockSpec((1, d, tn), lambda l, j: (l, 0, j)),
                  pl.BlockSpec((1, 1, tn), lambda l, j: (l, 0, j))],
        out_specs=pl.BlockSpec((1, 8, tn), lambda l, j: (l, 0, j)),
        out_shape=jax.ShapeDtypeStruct((depth, 8, d6), F32),
        name="ada_mod",
        compiler_params=_cparams("arbitrary", "arbitrary"),
    )(cc, ada_w, ada_b.reshape(depth, 1, d6))


def _in_kernel(x_ref, mod_ref, g1_ref, win_ref, qng_ref, wq_ref, kvg_ref, wkv_ref, cs1_ref, cs2_ref,
               hg_ref, q_ref, k_ref, vt_ref):
    m = mod_ref[0]
    h = _modulate(x_ref[0], g1_ref[...], m[0:1], m[1:2]).astype(BF16)
    proj = _dot(h, win_ref[...])
    o = 5 * HG_W
    hg_ref[0] = proj[:, :o]
    qa = proj[:, o:o + MLA_Q_RANK]
    kva = proj[:, o + MLA_Q_RANK:o + MLA_Q_RANK + MLA_KV_RANK]
    kp = proj[:, o + MLA_Q_RANK + MLA_KV_RANK:]
    qup = _dot(_rms(qa, qng_ref[...]).astype(BF16), wq_ref[...])
    kvup = _dot(_rms(kva, kvg_ref[...]).astype(BF16), wkv_ref[...])
    cs1 = cs1_ref[...]
    cs2 = cs2_ref[...]
    half = LANES // 2
    kpe = (kp * cs1 + pltpu.roll(kp, half, axis=1) * cs2).astype(BF16)
    nn = MLA_HEADS * MLA_NOPE
    qs = MLA_SCALE * LOG2E
    for hd in range(MLA_HEADS):
        p = qup[:, nn + LANES * hd:nn + LANES * (hd + 1)]
        qpe = p * cs1 + pltpu.roll(p, half, axis=1) * cs2
        q_ref[0, :, 2 * LANES * hd:2 * LANES * hd + LANES] = (qup[:, LANES * hd:LANES * (hd + 1)] * qs).astype(BF16)
        q_ref[0, :, 2 * LANES * hd + LANES:2 * LANES * (hd + 1)] = (qpe * qs).astype(BF16)
        k_ref[0, :, 2 * LANES * hd:2 * LANES * hd + LANES] = kvup[:, LANES * hd:LANES * (hd + 1)].astype(BF16)
        k_ref[0, :, 2 * LANES * hd + LANES:2 * LANES * (hd + 1)] = kpe
    vt_ref[0] = kvup[:, nn:].T.astype(BF16)


def _in_proj(x, mod, g1, win, qng, wq, kvg, wkv, cs1, cs2):
    b, n, d = x.shape
    tm = min(256, n)
    wcols = win.shape[1]
    full = lambda shape: pl.BlockSpec(shape, lambda i, j: (0,) * len(shape))
    return pl.pallas_call(
        _in_kernel,
        grid=(b, n // tm),
        in_specs=[pl.BlockSpec((1, tm, d), lambda i, j: (i, j, 0)),
                  pl.BlockSpec((1, 6, d), lambda i, j: (i, 0, 0)),
                  full((1, d)), full((d, wcols)), full((1, MLA_Q_RANK)), full(wq.shape),
                  full((1, MLA_KV_RANK)), full(wkv.shape),
                  pl.BlockSpec((tm, LANES), lambda i, j: (j, 0)),
                  pl.BlockSpec((tm, LANES), lambda i, j: (j, 0))],
        out_specs=[pl.BlockSpec((1, tm, 5 * HG_W), lambda i, j: (i, j, 0)),
                   pl.BlockSpec((1, tm, 2 * LANES * MLA_HEADS), lambda i, j: (i, j, 0)),
                   pl.BlockSpec((1, tm, 2 * LANES * MLA_HEADS), lambda i, j: (i, j, 0)),
                   pl.BlockSpec((1, MLA_HEADS * MLA_V, tm), lambda i, j: (i, 0, j))],
        out_shape=[jax.ShapeDtypeStruct((b, n, 5 * HG_W), F32),
                   jax.ShapeDtypeStruct((b, n, 2 * LANES * MLA_HEADS), BF16),
                   jax.ShapeDtypeStruct((b, n, 2 * LANES * MLA_HEADS), BF16),
                   jax.ShapeDtypeStruct((b, MLA_HEADS * MLA_V, n), BF16)],
        name="in_proj",
        compiler_params=_cparams("arbitrary", "arbitrary"),
    )(x, mod, g1, win, qng, wq, kvg, wkv, cs1, cs2)


def _attn_kernel(*refs, nparts):
    q_ref = refs[0]
    k_refs = refs[1:1 + nparts]
    v_refs = refs[1 + nparts:1 + 2 * nparts]
    o_ref = refs[1 + 2 * nparts]
    q = q_ref[0]
    tq = q.shape[0]
    s = [_nt(k[0], q) for k in k_refs]
    m = functools.reduce(jnp.maximum, [jnp.max(x, axis=0, keepdims=True) for x in s])
    p = [jnp.exp2(x - m) for x in s]
    den = functools.reduce(jnp.add, [jnp.sum(x, axis=0, keepdims=True) for x in p])
    acc = functools.reduce(jnp.add, [_dot(vt[0], x.astype(BF16)) for vt, x in zip(v_refs, p)])
    o_ref[0] = (acc / den).T.astype(o_ref.dtype)


def _attention(q, ks, vts):
    b, n, _ = q.shape
    tq = min(512, n)
    nparts = len(ks)
    in_specs = [pl.BlockSpec((1, tq, 2 * LANES), lambda i, h, j: (i, j, h))]
    in_specs += [pl.BlockSpec((1, k.shape[1], 2 * LANES), lambda i, h, j: (i, 0, h)) for k in ks]
    in_specs += [pl.BlockSpec((1, MLA_V, v.shape[2]), lambda i, h, j: (i, h, 0)) for v in vts]
    return pl.pallas_call(
        functools.partial(_attn_kernel, nparts=nparts),
        grid=(b, MLA_HEADS, n // tq),
        in_specs=in_specs,
        out_specs=pl.BlockSpec((1, tq, MLA_V), lambda i, h, j: (i, j, h)),
        out_shape=jax.ShapeDtypeStruct((b, n, MLA_HEADS * MLA_V), BF16),
        name="mla_attention",
        compiler_params=_cparams("arbitrary", "arbitrary", "arbitrary"),
    )(q, *ks, *vts)


_HG_HALVES = (1, 2, 4, 8, 16, 32)


def _hgrn_masks():
    t = np.arange(CHUNK)[:, None]
    s = np.arange(CHUNK)[None, :]
    masks_f = [(t == s)]
    for h in _HG_HALVES:
        masks_f.append((t // (2 * h) == s // (2 * h)) & (t % (2 * h) >= h) & (s % (2 * h) < h))
    masks_f = np.stack(masks_f).astype(np.float32)
    return masks_f, np.transpose(masks_f, (0, 2, 1))


def _hg_block_sums(logf, fw):
    c = CHUNK
    pre, suf = [logf], [jnp.zeros_like(logf)]
    h = 1
    while h < c:
        blk = 2 * h
        src_early = h - 1 if fw else h
        src_late = blk - 1 if fw else 0
        p = pre[-1]
        if blk <= SUBLANES:
            p3 = p.reshape(c // SUBLANES, SUBLANES, HG_D)
            r = lax.broadcasted_iota(I32, p3.shape, 1)
            tot_e = jnp.zeros_like(p3)
            tot_l = jnp.zeros_like(p3)
            for kb in range(SUBLANES // blk):
                inblk = jnp.right_shift(r, blk.bit_length() - 1) == kb
                tot_e = jnp.where(inblk, p3[:, kb * blk + src_early:kb * blk + src_early + 1, :], tot_e)
                tot_l = jnp.where(inblk, p3[:, kb * blk + src_late:kb * blk + src_late + 1, :], tot_l)
        else:
            p3 = p.reshape(c // blk, blk, HG_D)
            r = lax.broadcasted_iota(I32, p3.shape, 1)
            tot_e = jnp.broadcast_to(p3[:, src_early:src_early + 1, :], p3.shape)
            tot_l = jnp.broadcast_to(p3[:, src_late:src_late + 1, :], p3.shape)
        in_first = jnp.bitwise_and(r, blk - 1) < h
        late = jnp.logical_not(in_first) if fw else in_first
        s3 = suf[-1].reshape(p3.shape)
        pre.append(jnp.where(late, p3 + tot_e, p3).reshape(c, HG_D))
        suf.append(jnp.where(late, s3, s3 + tot_l).reshape(c, HG_D))
        h = blk
    return pre, suf


def _hg_chunk(q, z, v, loglb, log1mlb, onemlb, masks, st, fw, want_out):
    ls = jnp.minimum(z, 0.0) - jnp.log1p(jnp.exp(-jnp.abs(z)))
    bb = log1mlb + ls
    logf = jnp.maximum(loglb, bb) + jnp.log1p(jnp.exp(-jnp.abs(loglb - bb)))
    k = onemlb / (1.0 + jnp.exp(z))
    pre, suf = _hg_block_sums(logf, fw)
    vb = v.astype(BF16)
    whole = jnp.exp(pre[-1])
    kdl = (k * jnp.exp(suf[-1])).astype(BF16)
    last = CHUNK - 1 if fw else 0
    dec = whole[last:last + 1]
    out = None
    if want_out:
        kb = k.astype(BF16)
        att = jnp.where(masks[0] > 0.5, _nt(q.astype(BF16), kb), 0.0)
        for i in range(len(_HG_HALVES)):
            a = (q * jnp.exp(pre[i])).astype(BF16)
            bm = kb if i == 0 else (k * jnp.exp(suf[i])).astype(BF16)
            att = att + jnp.where(masks[1 + i] > 0.5, _nt(a, bm), 0.0)
        out = _dot(att.astype(BF16), vb) + _nt((q * whole).astype(BF16), st.astype(BF16))
    st_new = st * dec + _tn(vb, kdl)
    return out, st_new


def _hgrn_kernel(*refs, has_ctx, ctx_out):
    it = iter(refs)
    q_ref, zf_ref, zb_ref, v_ref, g_ref = (next(it) for _ in range(5))
    if has_ctx:
        qc_ref, zfc_ref, zbc_ref, vc_ref, gc_ref = (next(it) for _ in range(5))
    lb_ref, gn_ref, mkf_ref, mkb_ref = (next(it) for _ in range(4))
    o_ref = next(it)
    oc_ref = next(it) if ctx_out else None
    of_s, ob_s = next(it), next(it)
    if ctx_out:
        ocf_s, ocb_s = next(it), next(it)

    lbp = lb_ref[...]
    masks = (mkf_ref[...], mkb_ref[...])
    zrefs = (zf_ref, zb_ref)

    def seq_pass(q_r, z_rs, v_r, o_ss, n, states, want_out):
        nc = n // CHUNK

        def body(ci, carry):
            new = []
            for d in range(2):
                cidx = ci if d == 0 else nc - 1 - ci
                off = pl.multiple_of(cidx * CHUNK, CHUNK)
                rows = pl.ds(off, CHUNK)
                qv = q_r[0, rows, :] if want_out else None
                out, st = _hg_chunk(qv, z_rs[d][0, rows, :], v_r[0, rows, :],
                                    lbp[3 * d:3 * d + 1], lbp[3 * d + 1:3 * d + 2], lbp[3 * d + 2:3 * d + 3],
                                    masks[d], carry[d], d == 0, want_out)
                if want_out:
                    o_ss[d][rows, :] = out
                new.append(st)
            return tuple(new)

        return lax.fori_loop(0, nc, body, states, unroll=2)

    def readout(o_ss, g_r, out_r, n):
        tr = min(512, n)

        def body(i, _):
            rows = pl.ds(pl.multiple_of(i * tr, tr), tr)
            o = o_ss[0][rows, :] + o_ss[1][rows, :]
            g = g_r[0, rows, :]
            out_r[0, rows, :] = (_rms(o, gn_ref[...]) * (g * _sigmoid(g))).astype(out_r.dtype)
            return 0

        lax.fori_loop(0, n // tr, body, 0)

    zero = jnp.zeros((HG_D, HG_D), F32)
    states = (zero, zero)
    if has_ctx:
        nctx = vc_ref.shape[1]
        states = seq_pass(qc_ref, (zfc_ref, zbc_ref), vc_ref, (ocf_s, ocb_s) if ctx_out else None,
                          nctx, states, ctx_out)
        if ctx_out:
            readout((ocf_s, ocb_s), gc_ref, oc_ref, nctx)
    n = v_ref.shape[1]
    seq_pass(q_ref, zrefs, v_ref, (of_s, ob_s), n, states, True)
    readout((of_s, ob_s), g_ref, o_ref, n)


def _hgrn(hg, hg_ctx, lbp, gn, ctx_out):
    b, n, _ = hg.shape
    has_ctx = hg_ctx is not None
    consts = [jnp.asarray(a, F32) for a in _hgrn_masks()]

    def part(nn, k):
        return pl.BlockSpec((1, nn, HG_D), lambda i, h, k=k: (i, 0, k * HG_HEADS + h))

    in_specs = [part(n, k) for k in range(5)]
    args = [hg] * 5
    if has_ctx:
        nctx = hg_ctx.shape[1]
        in_specs += [part(nctx, k) for k in range(5)]
        args += [hg_ctx] * 5
    in_specs += [pl.BlockSpec((6, HG_D), lambda i, h: (0, h)),
                 pl.BlockSpec((1, HG_D), lambda i, h: (0, 0))]
    in_specs += [pl.BlockSpec(a.shape, lambda i, h, nd=a.ndim: (0,) * nd) for a in consts]
    args += [lbp, gn] + consts
    out_specs = [pl.BlockSpec((1, n, HG_D), lambda i, h: (i, 0, h))]
    out_shape = [jax.ShapeDtypeStruct((b, n, HG_W), BF16)]
    scratch = [pltpu.VMEM((n, HG_D), F32), pltpu.VMEM((n, HG_D), F32)]
    if ctx_out:
        out_specs.append(pl.BlockSpec((1, nctx, HG_D), lambda i, h: (i, 0, h)))
        out_shape.append(jax.ShapeDtypeStruct((b, nctx, HG_W), BF16))
        scratch += [pltpu.VMEM((nctx, HG_D), F32), pltpu.VMEM((nctx, HG_D), F32)]
    res = pl.pallas_call(
        functools.partial(_hgrn_kernel, has_ctx=has_ctx, ctx_out=ctx_out),
        grid=(b, HG_HEADS),
        in_specs=in_specs, out_specs=out_specs, out_shape=out_shape, scratch_shapes=scratch,
        name="hgrn_scan",
        compiler_params=_cparams("arbitrary", "arbitrary"),
    )(*args)
    return (res[0], res[1]) if ctx_out else (res[0], None)


def _out_kernel(hg_ref, mla_ref, x_ref, mod_ref, w_ref, o_ref):
    w = w_ref[...]
    y = _dot(hg_ref[0], w[:HG_W]) + _dot(mla_ref[0], w[HG_W:])
    o_ref[0] = x_ref[0] + mod_ref[0][2:3] * y


def _out_proj(hgo, mla, x, mod, w):
    b, n, d = x.shape
    tm = min(512, n)
    return pl.pallas_call(
        _out_kernel,
        grid=(b, n // tm),
        in_specs=[pl.BlockSpec((1, tm, HG_W), lambda i, j: (i, j, 0)),
                  pl.BlockSpec((1, tm, MLA_HEADS * MLA_V), lambda i, j: (i, j, 0)),
                  pl.BlockSpec((1, tm, d), lambda i, j: (i, j, 0)),
                  pl.BlockSpec((1, 6, d), lambda i, j: (i, 0, 0)),
                  pl.BlockSpec(w.shape, lambda i, j: (0, 0))],
        out_specs=pl.BlockSpec((1, tm, d), lambda i, j: (i, j, 0)),
        out_shape=jax.ShapeDtypeStruct((b, n, d), F32),
        name="out_proj",
        compiler_params=_cparams("arbitrary", "arbitrary"),
    )(hgo, mla, x, mod, w)


def _pool_kernel(x_ref, prev_ref, next_ref, mod_ref, g1_ref, pw_ref, ps_ref, o_ref, hs_ref, *, n):
    tm = x_ref.shape[1]
    d = x_ref.shape[2]
    gw = d // len(POOL_WINDOWS)
    m = mod_ref[0]
    base = pl.program_id(1) * tm
    xe = jnp.concatenate([prev_ref[0], x_ref[0], next_ref[0]], axis=0)
    he = _modulate(xe, g1_ref[...], m[0:1], m[1:2])
    gidx = base - POOL_HALO + lax.broadcasted_iota(I32, (tm + 2 * POOL_HALO, 1), 0)
    hs_ref[...] = jnp.where((gidx >= 0) & (gidx < n), he, 0.0)
    t = base + lax.broadcasted_iota(I32, (tm, 1), 0)
    ps = ps_ref[...]
    for gi, w in enumerate(POOL_WINDOWS):
        cols = slice(gi * gw, (gi + 1) * gw)
        acc = hs_ref[pl.ds(POOL_HALO - w // 2, tm), cols]
        for j in range(1 - w // 2, w // 2):
            acc = acc + hs_ref[pl.ds(POOL_HALO + j, tm), cols]
        lo = jnp.clip(t - w // 2, 0, n - 1)
        hi = jnp.clip(t + w // 2 - 1, 0, n - 1)
        pooled = acc / (hi - lo + 1).astype(F32) - hs_ref[pl.ds(POOL_HALO, tm), cols]
        y = _dot(pooled.astype(BF16), pw_ref[gi]) * ps[:, cols]
        o_ref[0, :, cols] = x_ref[0, :, cols] + m[2:3, cols] * y


def _pool(x, mod, g1, pw, ps):
    b, n, d = x.shape
    tm = min(512, n)
    nb = tm // POOL_HALO
    last = n // POOL_HALO - 1
    return pl.pallas_call(
        functools.partial(_pool_kernel, n=n),
        grid=(b, n // tm),
        in_specs=[pl.BlockSpec((1, tm, d), lambda i, j: (i, j, 0)),
                  pl.BlockSpec((1, POOL_HALO, d), lambda i, j: (i, jnp.maximum(j * nb - 1, 0), 0)),
                  pl.BlockSpec((1, POOL_HALO, d), lambda i, j: (i, jnp.minimum((j + 1) * nb, last), 0)),
                  pl.BlockSpec((1, 6, d), lambda i, j: (i, 0, 0)),
                  pl.BlockSpec((1, d), lambda i, j: (0, 0)),
                  pl.BlockSpec(pw.shape, lambda i, j: (0, 0, 0)),
                  pl.BlockSpec((1, d), lambda i, j: (0, 0))],
        out_specs=pl.BlockSpec((1, tm, d), lambda i, j: (i, j, 0)),
        out_shape=jax.ShapeDtypeStruct((b, n, d), F32),
        scratch_shapes=[pltpu.VMEM((tm + 2 * POOL_HALO, d), F32)],
        name="pool_mixer",
        compiler_params=_cparams("arbitrary", "arbitrary"),
    )(x, x, x, mod, g1, pw, ps)


def _route_kernel(x_ref, mod_ref, g2_ref, rt_ref, h_ref, aff_ref):
    m = mod_ref[0]
    h = _modulate(x_ref[0], g2_ref[...], m[3:4], m[4:5])
    hb = h.astype(BF16)
    h_ref[0] = hb
    hlo = (h - hb.astype(F32)).astype(BF16)
    rt = rt_ref[...]
    rhi = rt.astype(BF16)
    rlo = (rt - rhi.astype(F32)).astype(BF16)
    logits = _nt(rhi, hb) + (_nt(rhi, hlo) + _nt(rlo, hb))
    ex = jnp.exp(logits - jnp.max(logits, axis=0, keepdims=True))
    aff_ref[0] = ex / jnp.sum(ex, axis=0, keepdims=True)


def _route(x, mod, g2, rt):
    b, n, d = x.shape
    e = rt.shape[0]
    tm = min(512, n)
    return pl.pallas_call(
        _route_kernel,
        grid=(b, n // tm),
        in_specs=[pl.BlockSpec((1, tm, d), lambda i, j: (i, j, 0)),
                  pl.BlockSpec((1, 6, d), lambda i, j: (i, 0, 0)),
                  pl.BlockSpec((1, d), lambda i, j: (0, 0)),
                  pl.BlockSpec((e, d), lambda i, j: (0, 0))],
        out_specs=[pl.BlockSpec((1, tm, d), lambda i, j: (i, j, 0)),
                   pl.BlockSpec((1, e, tm), lambda i, j: (i, 0, j))],
        out_shape=[jax.ShapeDtypeStruct((b, n, d), BF16),
                   jax.ShapeDtypeStruct((b, e, n), F32)],
        name="moe_route",
        compiler_params=_cparams("arbitrary", "arbitrary"),
    )(x, mod, g2, rt)


def _topk_kernel(aff_ref, selpos_ref, st_ref, pm_ref, gm_ref, *, cap):
    aff = aff_ref[0]
    e, n = aff.shape
    nt = n // LANES
    capf = float(cap)

    def bit_step(i, tau):
        cand = tau | jnp.left_shift(jnp.int32(1), 30 - i)
        cnt = jnp.sum(jnp.where(aff >= pltpu.bitcast(cand, F32), 1.0, 0.0), axis=1, keepdims=True)
        return jnp.where(cnt >= capf, cand, tau)

    tau = lax.fori_loop(0, 31, bit_step, jnp.zeros((e, 1), I32))
    gt = aff >= pltpu.bitcast(tau + 1, F32)
    eq = (aff >= pltpu.bitcast(tau, F32)) & jnp.logical_not(gt)
    need = capf - jnp.sum(jnp.where(gt, 1.0, 0.0), axis=1, keepdims=True)

    r = lax.broadcasted_iota(I32, (LANES, LANES), 0)
    c = lax.broadcasted_iota(I32, (LANES, LANES), 1)
    upper = jnp.where(r < c, 1.0, 0.0).astype(BF16)
    eye = jnp.where(r == c, 1.0, 0.0).astype(BF16)

    def excl_prefix(mask_f):
        off = jnp.zeros((e, 1), F32)
        outs, starts = [], []
        for tb in range(nt):
            blk = mask_f[:, tb * LANES:(tb + 1) * LANES]
            outs.append(_dot(blk.astype(BF16), upper) + off)
            starts.append(off)
            off = off + jnp.sum(blk, axis=1, keepdims=True)
        return jnp.concatenate(outs, axis=1), starts

    eq_rank, _ = excl_prefix(jnp.where(eq, 1.0, 0.0))
    sel = gt | (eq & (eq_rank < need))
    pos, starts = excl_prefix(jnp.where(sel, 1.0, 0.0))
    selpos = jnp.where(sel, pos, -1.0)
    selpos_ref[0] = selpos.astype(I32)

    lane = lax.broadcasted_iota(I32, (e, nt), 1)
    st = jnp.zeros((e, nt), F32)
    for tb in range(nt):
        st = jnp.where(lane == tb, starts[tb], st)
    st_ref[0] = st.astype(I32)

    code = selpos + 1.0
    chi = jnp.floor(code * (1.0 / 16.0))
    clo = code - 16.0 * chi
    g3 = _split3(jnp.where(sel, aff, 0.0))
    for tb in range(nt):
        cols = slice(tb * LANES, (tb + 1) * LANES)
        tr = lambda a: _nt(eye, a[:, cols].astype(BF16))
        pm_ref[0, cols, :] = (16.0 * tr(chi) + tr(clo) - 1.0).astype(I32)
        gm_ref[0, cols, :] = tr(g3[0]) + tr(g3[1]) + tr(g3[2])


def _topk(aff_t, cap):
    b, e, n = aff_t.shape
    nt = n // LANES
    return pl.pallas_call(
        functools.partial(_topk_kernel, cap=cap),
        grid=(b,),
        in_specs=[pl.BlockSpec((1, e, n), lambda i: (i, 0, 0))],
        out_specs=[pl.BlockSpec((1, e, n), lambda i: (i, 0, 0)),
                   pl.BlockSpec((1, e, nt), lambda i: (i, 0, 0)),
                   pl.BlockSpec((1, n, e), lambda i: (i, 0, 0)),
                   pl.BlockSpec((1, n, e), lambda i: (i, 0, 0))],
        out_shape=[jax.ShapeDtypeStruct((b, e, n), I32),
                   jax.ShapeDtypeStruct((b, e, nt), I32),
                   jax.ShapeDtypeStruct((b, n, e), I32),
                   jax.ShapeDtypeStruct((b, n, e), F32)],
        name="moe_topk",
        compiler_params=_cparams("arbitrary"),
    )(aff_t)


def _gather_kernel(st_ref, selpos_ref, h_ref, xs_ref, acc_ref, *, cap):
    bi = pl.program_id(0)
    ei = pl.program_id(1)
    ne = pl.num_programs(1)
    n = h_ref.shape[1]
    nt = n // LANES
    base = (bi * ne + ei) * nt
    acc_ref[...] = jnp.zeros_like(acc_ref)

    def block_count(tb, mx):
        nxt = jnp.where(tb + 1 < nt, st_ref[base + jnp.minimum(tb + 1, nt - 1)], cap)
        return jnp.maximum(mx, nxt - st_ref[base + tb])

    most = lax.fori_loop(0, nt, block_count, jnp.int32(0))

    def run(win, unroll):
        riota = lax.broadcasted_iota(I32, (win, LANES), 0)

        def body(tb, _):
            p0 = pl.multiple_of(jnp.minimum(jnp.bitwise_and(st_ref[base + tb], -SUBLANES), cap - win), SUBLANES)
            rows = pl.ds(pl.multiple_of(tb * LANES, LANES), LANES)
            slot = selpos_ref[0, 0, pl.ds(tb, 1), :]
            onehot = jnp.where(slot - p0 == riota, 1.0, 0.0).astype(BF16)
            acc_ref[pl.ds(p0, win), :] += _dot(onehot, h_ref[0, rows, :])
            return 0

        lax.fori_loop(0, nt, body, 0, unroll=unroll)

    fast = min(GATHER_FAST_WIN, cap)
    full = min(GATHER_WIN, cap)

    @pl.when(most <= fast - SUBLANES)
    def _():
        run(fast, min(4, nt))

    @pl.when(most > fast - SUBLANES)
    def _():
        run(full, 1)

    xs_ref[0, 0] = acc_ref[...].astype(xs_ref.dtype)


def _gather(st, selpos, h2, cap):
    b, n, d = h2.shape
    e = selpos.shape[1]
    grid_spec = pltpu.PrefetchScalarGridSpec(
        num_scalar_prefetch=1,
        grid=(b, e),
        in_specs=[pl.BlockSpec((1, 1, n // LANES, LANES), lambda i, j, p: (i, j, 0, 0)),
                  pl.BlockSpec((1, n, d), lambda i, j, p: (i, 0, 0))],
        out_specs=pl.BlockSpec((1, 1, cap, d), lambda i, j, p: (j, i, 0, 0)),
        scratch_shapes=[pltpu.VMEM((cap, d), F32)],
    )
    return pl.pallas_call(
        functools.partial(_gather_kernel, cap=cap),
        grid_spec=grid_spec,
        out_shape=jax.ShapeDtypeStruct((e, b, cap, d), BF16),
        name="moe_gather",
        compiler_params=_cparams("arbitrary", "arbitrary"),
    )(st.reshape(-1), selpos.reshape(b, e, n // LANES, LANES), h2)


def _ffn_kernel(*refs, ngroups):
    xs = refs[:ngroups]
    wg_ref, wu_ref, wd_ref = refs[ngroups:ngroups + 3]
    ys = refs[ngroups + 3:2 * ngroups + 3]
    accs = refs[2 * ngroups + 3:]
    f = pl.program_id(1)
    wg = wg_ref[0, 0].astype(BF16)
    wu = wu_ref[0, 0].astype(BF16)
    wd = wd_ref[0, 0].astype(BF16)
    for x_ref, y_ref, acc_ref in zip(xs, ys, accs):
        rows = x_ref.shape[1]
        tr = min(512, rows)

        @pl.when(f == 0)
        def _(acc_ref=acc_ref):
            acc_ref[...] = jnp.zeros_like(acc_ref)

        def body(i, _, x_ref=x_ref, acc_ref=acc_ref, tr=tr):
            sl = pl.ds(pl.multiple_of(i * tr, tr), tr)
            x = x_ref[0, sl, :]
            g = _dot(x, wg)
            u = _dot(x, wu)
            acc_ref[sl, :] += _dot((g * _sigmoid(g) * u).astype(BF16), wd)
            return 0

        lax.fori_loop(0, rows // tr, body, 0)

        @pl.when(f == pl.num_programs(1) - 1)
        def _(y_ref=y_ref, acc_ref=acc_ref):
            y_ref[0] = acc_ref[...].astype(y_ref.dtype)


def _ffn(xs_list, wg, wu, wd, layer):
    _, e, d, ff = wg.shape
    tf = min(512, ff)
    ng = len(xs_list)
    in_specs = [pl.BlockSpec((1, x.shape[1], d), lambda i, j: (i, 0, 0)) for x in xs_list]
    in_specs += [pl.BlockSpec((1, 1, d, tf), lambda i, j: (layer, i, 0, j)),
                 pl.BlockSpec((1, 1, d, tf), lambda i, j: (layer, i, 0, j)),
                 pl.BlockSpec((1, 1, tf, d), lambda i, j: (layer, i, j, 0))]
    return pl.pallas_call(
        functools.partial(_ffn_kernel, ngroups=ng),
        grid=(e, ff // tf),
        in_specs=in_specs,
        out_specs=[pl.BlockSpec((1, x.shape[1], d), lambda i, j: (i, 0, 0)) for x in xs_list],
        out_shape=[jax.ShapeDtypeStruct(x.shape, BF16) for x in xs_list],
        scratch_shapes=[pltpu.VMEM((x.shape[1], d), F32) for x in xs_list],
        name="moe_ffn",
        compiler_params=_cparams("arbitrary", "arbitrary"),
    )(*xs_list, wg, wu, wd)


def _combine_kernel(st_ref, pm_ref, gm_ref, y_ref, x_ref, mod_ref, o_ref, *, cap, nt_total):
    bi = pl.program_id(0)
    tg = pl.program_id(2)
    ne = y_ref.shape[0]
    tm = x_ref.shape[1]
    sw = min(SCATTER_WIN, cap)
    liota = lax.broadcasted_iota(I32, (LANES, sw), 1)
    mgate = mod_ref[0][5:6]

    def body(t, _):
        tb = tg * (tm // LANES) + t
        rows = pl.ds(pl.multiple_of(t * LANES, LANES), LANES)
        pm = pm_ref[0, rows, :]
        gm = gm_ref[0, rows, :]
        acc = jnp.zeros((LANES, x_ref.shape[2]), F32)
        for ei in range(ne):
            start = st_ref[(bi * ne + ei) * nt_total + tb]
            p0 = pl.multiple_of(jnp.minimum(jnp.bitwise_and(start, -BF16_ROWS), cap - sw), BF16_ROWS)
            onehot = jnp.where(pm[:, ei:ei + 1] - p0 == liota, 1.0, 0.0).astype(BF16)
            acc = acc + gm[:, ei:ei + 1] * _dot(onehot, y_ref[ei, 0, pl.ds(p0, sw), :])
        o_ref[0, rows, :] = x_ref[0, rows, :] + mgate * acc
        return 0

    lax.fori_loop(0, tm // LANES, body, 0)


def _combine(st, pm, gm, y, x, mod, cap):
    b, n, d = x.shape
    e = y.shape[0]
    tm = min(512, n)
    dh = d // 2
    nt = n // LANES
    grid_spec = pltpu.PrefetchScalarGridSpec(
        num_scalar_prefetch=1,
        grid=(b, d // dh, n // tm),
        in_specs=[pl.BlockSpec((1, tm, e), lambda i, k, j, p: (i, j, 0)),
                  pl.BlockSpec((1, tm, e), lambda i, k, j, p: (i, j, 0)),
                  pl.BlockSpec((e, 1, cap, dh), lambda i, k, j, p: (0, i, 0, k)),
                  pl.BlockSpec((1, tm, dh), lambda i, k, j, p: (i, j, k)),
                  pl.BlockSpec((1, 6, dh), lambda i, k, j, p: (i, 0, k))],
        out_specs=pl.BlockSpec((1, tm, dh), lambda i, k, j, p: (i, j, k)),
    )
    return pl.pallas_call(
        functools.partial(_combine_kernel, cap=cap, nt_total=nt),
        grid_spec=grid_spec,
        out_shape=jax.ShapeDtypeStruct((b, n, d), F32),
        name="moe_combine",
        compiler_params=_cparams("arbitrary", "arbitrary", "arbitrary"),
    )(st.reshape(-1), pm, gm, y, x, mod)


def _final_kernel(x_ref, g_ref, o_ref):
    o_ref[0] = _rms(x_ref[0], g_ref[...])


def _final_norm(x, g):
    b, n, d = x.shape
    tm = min(1024, n)
    return pl.pallas_call(
        _final_kernel,
        grid=(b, n // tm),
        in_specs=[pl.BlockSpec((1, tm, d), lambda i, j: (i, j, 0)),
                  pl.BlockSpec((1, d), lambda i, j: (0, 0))],
        out_specs=pl.BlockSpec((1, tm, d), lambda i, j: (i, j, 0)),
        out_shape=jax.ShapeDtypeStruct((b, n, d), F32),
        name="final_norm",
        compiler_params=_cparams("arbitrary", "arbitrary"),
    )(x, g)


def _rot_cols(w):
    q = MLA_ROPE // 4
    return jnp.concatenate([-w[:, q:2 * q], w[:, :q], -w[:, 3 * q:], w[:, 2 * q:3 * q]], axis=1)


def _prep_even(w_in, wq_b, wkv_b):
    kpe = w_in[:, -MLA_ROPE:]
    win = jnp.concatenate([w_in, _rot_cols(kpe)], axis=1).astype(BF16)
    qh = wq_b.reshape(MLA_Q_RANK, MLA_HEADS, MLA_NOPE + MLA_ROPE)
    nope = [qh[:, h, :MLA_NOPE] for h in range(MLA_HEADS)]
    pe = [jnp.concatenate([qh[:, h, MLA_NOPE:], _rot_cols(qh[:, h, MLA_NOPE:])], axis=1) for h in range(MLA_HEADS)]
    wq = jnp.concatenate(nope + pe, axis=1).astype(BF16)
    kvh = wkv_b.reshape(MLA_KV_RANK, MLA_HEADS, MLA_NOPE + MLA_V)
    wkv = jnp.concatenate([kvh[:, h, :MLA_NOPE] for h in range(MLA_HEADS)]
                          + [kvh[:, h, MLA_NOPE:] for h in range(MLA_HEADS)], axis=1).astype(BF16)
    return win, wq, wkv


def _rope_tables(n):
    rows = n // GRID_W
    row = jnp.repeat(jnp.arange(rows, dtype=F32), GRID_W)
    col = jnp.tile(jnp.arange(GRID_W, dtype=F32), rows)
    half = MLA_ROPE // 2
    inv = 1.0 / (ROPE_THETA ** (jnp.arange(0, half, 2, dtype=F32) / half))
    ar = row[:, None] * inv[None, :]
    ac = col[:, None] * inv[None, :]
    ang = jnp.concatenate([ar, ar, ac, ac], axis=-1)
    z = jnp.zeros((n, LANES - MLA_ROPE), F32)
    return jnp.concatenate([jnp.cos(ang), z], axis=1), jnp.concatenate([jnp.sin(ang), z], axis=1)


def _moe(x, mod, g2, rt, wg, wu, wd, layer, xc=None, modc=None):
    streams = [(x, mod)] + ([(xc, modc)] if xc is not None else [])
    routed = []
    for xx, mm in streams:
        b, n, d = xx.shape
        e = rt.shape[0]
        cap = EC_FACTOR * n // e
        h2, aff_t = _route(xx, mm, g2, rt)
        selpos, st, pm, gm = _topk(aff_t, cap)
        xs = _gather(st, selpos, h2, cap)
        routed.append((st, pm, gm, cap, xs))
    ys = _ffn([r[4].reshape(r[4].shape[0], -1, r[4].shape[3]) for r in routed], wg, wu, wd, layer)
    outs = []
    for (xx, mm), (st, pm, gm, cap, xs), y in zip(streams, routed, ys):
        outs.append(_combine(st, pm, gm, y.reshape(xs.shape), xx, mm, cap))
    return outs if xc is not None else outs[0]


def kernel(x, c, ctx, c_ctx, ada_w, ada_b, norm1_g, norm2_g, w_in, hg_lb, hg_norm_g, mla_qn_g, mla_wq_b,
           mla_kvn_g, mla_wkv_b, w_out, pool_w, pool_scale, router_w, exp_wg, exp_wu, exp_wd, final_g):
    b, n, d = x.shape
    nctx = ctx.shape[1]
    depth = ada_w.shape[0]
    n_even = w_in.shape[0]
    assert b + 1 <= 8 and n % LANES == 0 and nctx % LANES == 0 and n % GRID_W == 0

    cc = jnp.zeros((8, d), F32).at[:b].set(c).at[b].set(c_ctx)
    mods = _ada(cc, ada_w, ada_b)
    mod_lat = [mods[l, :b].reshape(b, 6, d) for l in range(depth)]
    mod_ctx = [jnp.broadcast_to(mods[l, b].reshape(1, 6, d), (b, 6, d)) for l in range(depth)]

    lb_all = jnp.cumsum(jax.nn.softmax(hg_lb.astype(F32), axis=0), axis=0)
    lb_all = lb_all - lb_all[:1]
    cos1, sin1 = _rope_tables(n)
    cos_c = jnp.concatenate([jnp.ones((nctx, MLA_ROPE), F32), jnp.zeros((nctx, LANES - MLA_ROPE), F32)], axis=1)
    sin_c = jnp.zeros((nctx, LANES), F32)

    last_reader = 2 * (n_even - 1)
    row = lambda v: v.reshape(1, -1)
    x_lat, x_ctx = x, ctx
    for l in range(depth):
        j = l // 2
        ctx_in = l <= last_reader
        ctx_out = l < last_reader
        if l % 2 == 0:
            win, wq, wkv = _prep_even(w_in[j], mla_wq_b[j], mla_wkv_b[j])
            lb = lb_all[j]
            lbp = jnp.stack([jnp.log(lb[0]), jnp.log1p(-lb[0]), 1.0 - lb[0],
                             jnp.log(lb[1]), jnp.log1p(-lb[1]), 1.0 - lb[1]])
            common = (row(norm1_g[l]), win, row(mla_qn_g[j]), wq, row(mla_kvn_g[j]), wkv)
            hg_l, q_l, k_l, v_l = _in_proj(x_lat, mod_lat[l], *common, cos1, sin1)
            hg_c = None
            ks, vs = [k_l], [v_l]
            if ctx_in:
                hg_c, q_c, k_c, v_c = _in_proj(x_ctx, mod_ctx[l], *common, cos_c, sin_c)
                ks, vs = [k_c, k_l], [v_c, v_l]
            hgo_l, hgo_c = _hgrn(hg_l, hg_c, lbp, row(hg_norm_g[j]), ctx_out)
            mla_l = _attention(q_l, ks, vs)
            wo = w_out[j].astype(BF16)
            x_lat = _out_proj(hgo_l, mla_l, x_lat, mod_lat[l], wo)
            if ctx_out:
                mla_c = _attention(q_c, [k_c], [v_c])
                x_ctx = _out_proj(hgo_c, mla_c, x_ctx, mod_ctx[l], wo)
        else:
            pw = pool_w[j].astype(BF16)
            x_lat = _pool(x_lat, mod_lat[l], row(norm1_g[l]), pw, row(pool_scale[j]))
            if ctx_out:
                x_ctx = _pool(x_ctx, mod_ctx[l], row(norm1_g[l]), pw, row(pool_scale[j]))
        rt = router_w[l].T
        if ctx_out:
            x_lat, x_ctx = _moe(x_lat, mod_lat[l], row(norm2_g[l]), rt, exp_wg, exp_wu, exp_wd, l,
                                x_ctx, mod_ctx[l])
        else:
            x_lat = _moe(x_lat, mod_lat[l], row(norm2_g[l]), rt, exp_wg, exp_wu, exp_wd, l)
    return _final_norm(x_lat, row(final_g))
```

```python
import functools

import numpy as np
import jax
import jax.numpy as jnp
from jax import lax
from jax.experimental import pallas as pl
from jax.experimental.pallas import tpu as pltpu

F32 = jnp.float32
BF16 = jnp.bfloat16
I32 = jnp.int32

EPS = 1e-6
GRID_W = 64
ROPE_THETA = 10000.0
HG_HEADS = 4
HG_D = 128
MLA_HEADS = 4
MLA_Q_RANK = 256
MLA_KV_RANK = 128
MLA_NOPE = 128
MLA_ROPE = 64
MLA_V = 128
MLA_SCALE = (MLA_NOPE + MLA_ROPE) ** -0.5
POOL_WINDOWS = (2, 4, 8, 16)
EC_FACTOR = 2
HG_W = HG_HEADS * HG_D

LANES = 128
CHUNK = 64
POOL_HALO = 8
SUBLANES = 8
BF16_ROWS = 16
GATHER_WIN = LANES + BF16_ROWS
GATHER_FAST_WIN = 48
SCATTER_WIN = 256
SCATTER_FAST_WIN = 64
ATTN_KEY_CHUNK = 512
LOG2E = 1.4426950408889634
VMEM_LIMIT_V7X = 56 * 1024 * 1024


def _cparams(*sem):
    return pltpu.CompilerParams(dimension_semantics=sem, vmem_limit_bytes=VMEM_LIMIT_V7X)


def _nt(a, b):
    return lax.dot_general(a, b, (((1,), (1,)), ((), ())), preferred_element_type=F32)


def _tn(a, b):
    return lax.dot_general(a, b, (((0,), (0,)), ((), ())), preferred_element_type=F32)


def _dot(a, b):
    return jnp.dot(a, b, preferred_element_type=F32)


def _sigmoid(x):
    return 1.0 / (1.0 + jnp.exp(-x))


def _rms(x, g):
    return x * lax.rsqrt(jnp.mean(x * x, axis=-1, keepdims=True) + EPS) * g


def _modulate(x, g, shift, scale):
    return _rms(x, g) * (1.0 + scale) + shift


def _split3(x):
    hi = x.astype(BF16)
    r1 = x - hi.astype(F32)
    mid = r1.astype(BF16)
    lo = (r1 - mid.astype(F32)).astype(BF16)
    return hi, mid, lo


def _ada_kernel(c_ref, w_ref, b_ref, o_ref):
    c = c_ref[...]
    s = (c * _sigmoid(c)).astype(BF16)
    o_ref[0] = _dot(s, w_ref[0].astype(BF16)) + b_ref[0]


def _ada(cc, ada_w, ada_b):
    depth, d, d6 = ada_w.shape
    tn = d6 // 4
    return pl.pallas_call(
        _ada_kernel,
        grid=(depth, d6 // tn),
        in_specs=[pl.BlockSpec((8, d), lambda l, j: (0, 0)),
                  pl.BlockSpec((1, d, tn), lambda l, j: (l, 0, j)),
                  pl.BlockSpec((1, 1, tn), lambda l, j: (l, 0, j))],
        out_specs=pl.BlockSpec((1, 8, tn), lambda l, j: (l, 0, j)),
        out_shape=jax.ShapeDtypeStruct((depth, 8, d6), F32),
        name="ada_mod",
        compiler_params=_cparams("arbitrary", "arbitrary"),
    )(cc, ada_w, ada_b.reshape(depth, 1, d6))


def _in_kernel(x_ref, mod_ref, g1_ref, win_ref, qng_ref, wq_ref, kvg_ref, wkv_ref, cs1_ref, cs2_ref,
               hg_ref, q_ref, k_ref, vt_ref):
    m = mod_ref[0]
    h = _modulate(x_ref[0], g1_ref[...], m[0:1], m[1:2]).astype(BF16)
    proj = _dot(h, win_ref[...])
    o = 5 * HG_W
    hg_ref[0] = proj[:, :o]
    qa = proj[:, o:o + MLA_Q_RANK]
    kva = proj[:, o + MLA_Q_RANK:o + MLA_Q_RANK + MLA_KV_RANK]
    kp = proj[:, o + MLA_Q_RANK + MLA_KV_RANK:]
    qup = _dot(_rms(qa, qng_ref[...]).astype(BF16), wq_ref[...])
    kvup = _dot(_rms(kva, kvg_ref[...]).astype(BF16), wkv_ref[...])
    cs1 = cs1_ref[...]
    cs2 = cs2_ref[...]
    half = LANES // 2
    kpe = (kp * cs1 + pltpu.roll(kp, half, axis=1) * cs2).astype(BF16)
    nn = MLA_HEADS * MLA_NOPE
    qs = MLA_SCALE * LOG2E
    for hd in range(MLA_HEADS):
        p = qup[:, nn + LANES * hd:nn + LANES * (hd + 1)]
        qpe = p * cs1 + pltpu.roll(p, half, axis=1) * cs2
        q_ref[0, :, 2 * LANES * hd:2 * LANES * hd + LANES] = (qup[:, LANES * hd:LANES * (hd + 1)] * qs).astype(BF16)
        q_ref[0, :, 2 * LANES * hd + LANES:2 * LANES * (hd + 1)] = (qpe * qs).astype(BF16)
        k_ref[0, :, 2 * LANES * hd:2 * LANES * hd + LANES] = kvup[:, LANES * hd:LANES * (hd + 1)].astype(BF16)
        k_ref[0, :, 2 * LANES * hd + LANES:2 * LANES * (hd + 1)] = kpe
    vt_ref[0] = kvup[:, nn:].T.astype(BF16)


def _in_proj(x, mod, g1, win, qng, wq, kvg, wkv, cs1, cs2):
    b, n, d = x.shape
    tm = min(256, n)
    wcols = win.shape[1]
    full = lambda shape: pl.BlockSpec(shape, lambda i, j: (0,) * len(shape))
    return pl.pallas_call(
        _in_kernel,
        grid=(b, n // tm),
        in_specs=[pl.BlockSpec((1, tm, d), lambda i, j: (i, j, 0)),
                  pl.BlockSpec((1, 6, d), lambda i, j: (i, 0, 0)),
                  full((1, d)), full((d, wcols)), full((1, MLA_Q_RANK)), full(wq.shape),
                  full((1, MLA_KV_RANK)), full(wkv.shape),
                  pl.BlockSpec((tm, LANES), lambda i, j: (j, 0)),
                  pl.BlockSpec((tm, LANES), lambda i, j: (j, 0))],
        out_specs=[pl.BlockSpec((1, tm, 5 * HG_W), lambda i, j: (i, j, 0)),
                   pl.BlockSpec((1, tm, 2 * LANES * MLA_HEADS), lambda i, j: (i, j, 0)),
                   pl.BlockSpec((1, tm, 2 * LANES * MLA_HEADS), lambda i, j: (i, j, 0)),
                   pl.BlockSpec((1, MLA_HEADS * MLA_V, tm), lambda i, j: (i, 0, j))],
        out_shape=[jax.ShapeDtypeStruct((b, n, 5 * HG_W), F32),
                   jax.ShapeDtypeStruct((b, n, 2 * LANES * MLA_HEADS), BF16),
                   jax.ShapeDtypeStruct((b, n, 2 * LANES * MLA_HEADS), BF16),
                   jax.ShapeDtypeStruct((b, MLA_HEADS * MLA_V, n), BF16)],
        name="in_proj",
        compiler_params=_cparams("arbitrary", "arbitrary"),
    )(x, mod, g1, win, qng, wq, kvg, wkv, cs1, cs2)


def _attn_kernel(*refs, nparts):
    q_ref = refs[0]
    k_refs = refs[1:1 + nparts]
    v_refs = refs[1 + nparts:1 + 2 * nparts]
    o_ref = refs[1 + 2 * nparts]
    q = q_ref[0]
    tq = q.shape[0]
    s = [_nt(k[0], q) for k in k_refs]
    m = functools.reduce(jnp.maximum, [jnp.max(x, axis=0, keepdims=True) for x in s])
    p = [jnp.exp2(x - m) for x in s]
    den = functools.reduce(jnp.add, [jnp.sum(x, axis=0, keepdims=True) for x in p])
    acc = functools.reduce(jnp.add, [_dot(vt[0], x.astype(BF16)) for vt, x in zip(v_refs, p)])
    o_ref[0] = (acc / den).T.astype(o_ref.dtype)


def _attention(q, ks, vts):
    b, n, _ = q.shape
    tq = min(512, n)
    nparts = len(ks)
    in_specs = [pl.BlockSpec((1, tq, 2 * LANES), lambda i, h, j: (i, j, h))]
    in_specs += [pl.BlockSpec((1, k.shape[1], 2 * LANES), lambda i, h, j: (i, 0, h)) for k in ks]
    in_specs += [pl.BlockSpec((1, MLA_V, v.shape[2]), lambda i, h, j: (i, h, 0)) for v in vts]
    return pl.pallas_call(
        functools.partial(_attn_kernel, nparts=nparts),
        grid=(b, MLA_HEADS, n // tq),
        in_specs=in_specs,
        out_specs=pl.BlockSpec((1, tq, MLA_V), lambda i, h, j: (i, j, h)),
        out_shape=jax.ShapeDtypeStruct((b, n, MLA_HEADS * MLA_V), BF16),
        name="mla_attention",
        compiler_params=_cparams("arbitrary", "arbitrary", "arbitrary"),
    )(q, *ks, *vts)


_HG_HALVES = (1, 2, 4, 8, 16, 32)


def _hgrn_masks():
    t = np.arange(CHUNK)[:, None]
    s = np.arange(CHUNK)[None, :]
    masks_f = [(t == s)]
    for h in _HG_HALVES:
        masks_f.append((t // (2 * h) == s // (2 * h)) & (t % (2 * h) >= h) & (s % (2 * h) < h))
    masks_f = np.stack(masks_f).astype(np.float32)
    return masks_f, np.transpose(masks_f, (0, 2, 1))


def _hg_block_sums(logf, fw):
    c = CHUNK
    pre, suf = [logf], [jnp.zeros_like(logf)]
    h = 1
    while h < c:
        blk = 2 * h
        src_early = h - 1 if fw else h
        src_late = blk - 1 if fw else 0
        p = pre[-1]
        if blk <= SUBLANES:
            p3 = p.reshape(c // SUBLANES, SUBLANES, HG_D)
            r = lax.broadcasted_iota(I32, p3.shape, 1)
            tot_e = jnp.zeros_like(p3)
            tot_l = jnp.zeros_like(p3)
            for kb in range(SUBLANES // blk):
                inblk = jnp.right_shift(r, blk.bit_length() - 1) == kb
                tot_e = jnp.where(inblk, p3[:, kb * blk + src_early:kb * blk + src_early + 1, :], tot_e)
                tot_l = jnp.where(inblk, p3[:, kb * blk + src_late:kb * blk + src_late + 1, :], tot_l)
        else:
            p3 = p.reshape(c // blk, blk, HG_D)
            r = lax.broadcasted_iota(I32, p3.shape, 1)
            tot_e = jnp.broadcast_to(p3[:, src_early:src_early + 1, :], p3.shape)
            tot_l = jnp.broadcast_to(p3[:, src_late:src_late + 1, :], p3.shape)
        in_first = jnp.bitwise_and(r, blk - 1) < h
        late = jnp.logical_not(in_first) if fw else in_first
        s3 = suf[-1].reshape(p3.shape)
        pre.append(jnp.where(late, p3 + tot_e, p3).reshape(c, HG_D))
        suf.append(jnp.where(late, s3, s3 + tot_l).reshape(c, HG_D))
        h = blk
    return pre, suf


def _hg_chunk(q, z, v, loglb, log1mlb, onemlb, masks, st, fw, want_out):
    ls = jnp.minimum(z, 0.0) - jnp.log1p(jnp.exp(-jnp.abs(z)))
    bb = log1mlb + ls
    logf = jnp.maximum(loglb, bb) + jnp.log1p(jnp.exp(-jnp.abs(loglb - bb)))
    k = onemlb / (1.0 + jnp.exp(z))
    pre, suf = _hg_block_sums(logf, fw)
    vb = v.astype(BF16)
    whole = jnp.exp(pre[-1])
    kdl = (k * jnp.exp(suf[-1])).astype(BF16)
    last = CHUNK - 1 if fw else 0
    dec = whole[last:last + 1]
    out = None
    if want_out:
        kb = k.astype(BF16)
        att = jnp.where(masks[0] > 0.5, _nt(q.astype(BF16), kb), 0.0)
        for i in range(len(_HG_HALVES)):
            a = (q * jnp.exp(pre[i])).astype(BF16)
            bm = kb if i == 0 else (k * jnp.exp(suf[i])).astype(BF16)
            att = att + jnp.where(masks[1 + i] > 0.5, _nt(a, bm), 0.0)
        out = _dot(att.astype(BF16), vb) + _nt((q * whole).astype(BF16), st.astype(BF16))
    st_new = st * dec + _tn(vb, kdl)
    return out, st_new


def _hgrn_kernel(*refs, has_ctx, ctx_out):
    it = iter(refs)
    q_ref, zf_ref, zb_ref, v_ref, g_ref = (next(it) for _ in range(5))
    if has_ctx:
        qc_ref, zfc_ref, zbc_ref, vc_ref, gc_ref = (next(it) for _ in range(5))
    lb_ref, gn_ref, mkf_ref, mkb_ref = (next(it) for _ in range(4))
    o_ref = next(it)
    oc_ref = next(it) if ctx_out else None
    of_s, ob_s = next(it), next(it)
    if ctx_out:
        ocf_s, ocb_s = next(it), next(it)

    lbp = lb_ref[...]
    masks = (mkf_ref[...], mkb_ref[...])
    zrefs = (zf_ref, zb_ref)

    def seq_pass(q_r, z_rs, v_r, o_ss, n, states, want_out):
        nc = n // CHUNK

        def body(ci, carry):
            new = []
            for d in range(2):
                cidx = ci if d == 0 else nc - 1 - ci
                off = pl.multiple_of(cidx * CHUNK, CHUNK)
                rows = pl.ds(off, CHUNK)
                qv = q_r[0, rows, :] if want_out else None
                out, st = _hg_chunk(qv, z_rs[d][0, rows, :], v_r[0, rows, :],
                                    lbp[3 * d:3 * d + 1], lbp[3 * d + 1:3 * d + 2], lbp[3 * d + 2:3 * d + 3],
                                    masks[d], carry[d], d == 0, want_out)
                if want_out:
                    o_ss[d][rows, :] = out
                new.append(st)
            return tuple(new)

        return lax.fori_loop(0, nc, body, states, unroll=2)

    def readout(o_ss, g_r, out_r, n):
        tr = min(512, n)

        def body(i, _):
            rows = pl.ds(pl.multiple_of(i * tr, tr), tr)
            o = o_ss[0][rows, :] + o_ss[1][rows, :]
            g = g_r[0, rows, :]
            out_r[0, rows, :] = (_rms(o, gn_ref[...]) * (g * _sigmoid(g))).astype(out_r.dtype)
            return 0

        lax.fori_loop(0, n // tr, body, 0)

    zero = jnp.zeros((HG_D, HG_D), F32)
    states = (zero, zero)
    if has_ctx:
        nctx = vc_ref.shape[1]
        states = seq_pass(qc_ref, (zfc_ref, zbc_ref), vc_ref, (ocf_s, ocb_s) if ctx_out else None,
                          nctx, states, ctx_out)
        if ctx_out:
            readout((ocf_s, ocb_s), gc_ref, oc_ref, nctx)
    n = v_ref.shape[1]
    seq_pass(q_ref, zrefs, v_ref, (of_s, ob_s), n, states, True)
    readout((of_s, ob_s), g_ref, o_ref, n)


def _hgrn(hg, hg_ctx, lbp, gn, ctx_out):
    b, n, _ = hg.shape
    has_ctx = hg_ctx is not None
    consts = [jnp.asarray(a, F32) for a in _hgrn_masks()]

    def part(nn, k):
        return pl.BlockSpec((1, nn, HG_D), lambda i, h, k=k: (i, 0, k * HG_HEADS + h))

    in_specs = [part(n, k) for k in range(5)]
    args = [hg] * 5
    if has_ctx:
        nctx = hg_ctx.shape[1]
        in_specs += [part(nctx, k) for k in range(5)]
        args += [hg_ctx] * 5
    in_specs += [pl.BlockSpec((6, HG_D), lambda i, h: (0, h)),
                 pl.BlockSpec((1, HG_D), lambda i, h: (0, 0))]
    in_specs += [pl.BlockSpec(a.shape, lambda i, h, nd=a.ndim: (0,) * nd) for a in consts]
    args += [lbp, gn] + consts
    out_specs = [pl.BlockSpec((1, n, HG_D), lambda i, h: (i, 0, h))]
    out_shape = [jax.ShapeDtypeStruct((b, n, HG_W), BF16)]
    scratch = [pltpu.VMEM((n, HG_D), F32), pltpu.VMEM((n, HG_D), F32)]
    if ctx_out:
        out_specs.append(pl.BlockSpec((1, nctx, HG_D), lambda i, h: (i, 0, h)))
        out_shape.append(jax.ShapeDtypeStruct((b, nctx, HG_W), BF16))
        scratch += [pltpu.VMEM((nctx, HG_D), F32), pltpu.VMEM((nctx, HG_D), F32)]
    res = pl.pallas_call(
        functools.partial(_hgrn_kernel, has_ctx=has_ctx, ctx_out=ctx_out),
        grid=(b, HG_HEADS),
        in_specs=in_specs, out_specs=out_specs, out_shape=out_shape, scratch_shapes=scratch,
        name="hgrn_scan",
        compiler_params=_cparams("arbitrary", "arbitrary"),
    )(*args)
    return (res[0], res[1]) if ctx_out else (res[0], None)


def _out_kernel(hg_ref, mla_ref, x_ref, mod_ref, w_ref, o_ref):
    w = w_ref[...]
    y = _dot(hg_ref[0], w[:HG_W]) + _dot(mla_ref[0], w[HG_W:])
    o_ref[0] = x_ref[0] + mod_ref[0][2:3] * y


def _out_proj(hgo, mla, x, mod, w):
    b, n, d = x.shape
    tm = min(512, n)
    return pl.pallas_call(
        _out_kernel,
        grid=(b, n // tm),
        in_specs=[pl.BlockSpec((1, tm, HG_W), lambda i, j: (i, j, 0)),
                  pl.BlockSpec((1, tm, MLA_HEADS * MLA_V), lambda i, j: (i, j, 0)),
                  pl.BlockSpec((1, tm, d), lambda i, j: (i, j, 0)),
                  pl.BlockSpec((1, 6, d), lambda i, j: (i, 0, 0)),
                  pl.BlockSpec(w.shape, lambda i, j: (0, 0))],
        out_specs=pl.BlockSpec((1, tm, d), lambda i, j: (i, j, 0)),
        out_shape=jax.ShapeDtypeStruct((b, n, d), F32),
        name="out_proj",
        compiler_params=_cparams("arbitrary", "arbitrary"),
    )(hgo, mla, x, mod, w)


def _pool_kernel(x_ref, prev_ref, next_ref, mod_ref, g1_ref, pw_ref, ps_ref, o_ref, hs_ref, *, n):
    tm = x_ref.shape[1]
    d = x_ref.shape[2]
    gw = d // len(POOL_WINDOWS)
    m = mod_ref[0]
    base = pl.program_id(1) * tm
    xe = jnp.concatenate([prev_ref[0], x_ref[0], next_ref[0]], axis=0)
    he = _modulate(xe, g1_ref[...], m[0:1], m[1:2])
    gidx = base - POOL_HALO + lax.broadcasted_iota(I32, (tm + 2 * POOL_HALO, 1), 0)
    hs_ref[...] = jnp.where((gidx >= 0) & (gidx < n), he, 0.0)
    t = base + lax.broadcasted_iota(I32, (tm, 1), 0)
    ps = ps_ref[...]
    for gi, w in enumerate(POOL_WINDOWS):
        cols = slice(gi * gw, (gi + 1) * gw)
        acc = hs_ref[pl.ds(POOL_HALO - w // 2, tm), cols]
        for j in range(1 - w // 2, w // 2):
            acc = acc + hs_ref[pl.ds(POOL_HALO + j, tm), cols]
        lo = jnp.clip(t - w // 2, 0, n - 1)
        hi = jnp.clip(t + w // 2 - 1, 0, n - 1)
        pooled = acc / (hi - lo + 1).astype(F32) - hs_ref[pl.ds(POOL_HALO, tm), cols]
        y = _dot(pooled.astype(BF16), pw_ref[gi]) * ps[:, cols]
        o_ref[0, :, cols] = x_ref[0, :, cols] + m[2:3, cols] * y


def _pool(x, mod, g1, pw, ps):
    b, n, d = x.shape
    tm = min(512, n)
    nb = tm // POOL_HALO
    last = n // POOL_HALO - 1
    return pl.pallas_call(
        functools.partial(_pool_kernel, n=n),
        grid=(b, n // tm),
        in_specs=[pl.BlockSpec((1, tm, d), lambda i, j: (i, j, 0)),
                  pl.BlockSpec((1, POOL_HALO, d), lambda i, j: (i, jnp.maximum(j * nb - 1, 0), 0)),
                  pl.BlockSpec((1, POOL_HALO, d), lambda i, j: (i, jnp.minimum((j + 1) * nb, last), 0)),
                  pl.BlockSpec((1, 6, d), lambda i, j: (i, 0, 0)),
                  pl.BlockSpec((1, d), lambda i, j: (0, 0)),
                  pl.BlockSpec(pw.shape, lambda i, j: (0, 0, 0)),
                  pl.BlockSpec((1, d), lambda i, j: (0, 0))],
        out_specs=pl.BlockSpec((1, tm, d), lambda i, j: (i, j, 0)),
        out_shape=jax.ShapeDtypeStruct((b, n, d), F32),
        scratch_shapes=[pltpu.VMEM((tm + 2 * POOL_HALO, d), F32)],
        name="pool_mixer",
        compiler_params=_cparams("arbitrary", "arbitrary"),
    )(x, x, x, mod, g1, pw, ps)


def _route_kernel(x_ref, mod_ref, g2_ref, rt_ref, h_ref, aff_ref):
    m = mod_ref[0]
    h = _modulate(x_ref[0], g2_ref[...], m[3:4], m[4:5])
    hb = h.astype(BF16)
    h_ref[0] = hb
    hlo = (h - hb.astype(F32)).astype(BF16)
    rt = rt_ref[...]
    rhi = rt.astype(BF16)
    rlo = (rt - rhi.astype(F32)).astype(BF16)
    logits = _nt(rhi, hb) + (_nt(rhi, hlo) + _nt(rlo, hb))
    ex = jnp.exp(logits - jnp.max(logits, axis=0, keepdims=True))
    aff_ref[0] = ex / jnp.sum(ex, axis=0, keepdims=True)


def _route(x, mod, g2, rt):
    b, n, d = x.shape
    e = rt.shape[0]
    tm = min(512, n)
    return pl.pallas_call(
        _route_kernel,
        grid=(b, n // tm),
        in_specs=[pl.BlockSpec((1, tm, d), lambda i, j: (i, j, 0)),
                  pl.BlockSpec((1, 6, d), lambda i, j: (i, 0, 0)),
                  pl.BlockSpec((1, d), lambda i, j: (0, 0)),
                  pl.BlockSpec((e, d), lambda i, j: (0, 0))],
        out_specs=[pl.BlockSpec((1, tm, d), lambda i, j: (i, j, 0)),
                   pl.BlockSpec((1, e, tm), lambda i, j: (i, 0, j))],
        out_shape=[jax.ShapeDtypeStruct((b, n, d), BF16),
                   jax.ShapeDtypeStruct((b, e, n), F32)],
        name="moe_route",
        compiler_params=_cparams("arbitrary", "arbitrary"),
    )(x, mod, g2, rt)


def _topk_kernel(aff_ref, selpos_ref, gsel_ref, st_ref, stv_ref, most_ref, pm_ref, *, cap):
    aff = aff_ref[0]
    e, n = aff.shape
    nt = n // LANES
    capf = float(cap)

    def bit_step(i, tau):
        cand = tau | jnp.left_shift(jnp.int32(1), 30 - i)
        cnt = jnp.sum(jnp.where(aff >= pltpu.bitcast(cand, F32), 1.0, 0.0), axis=1, keepdims=True)
        return jnp.where(cnt >= capf, cand, tau)

    tau = lax.fori_loop(0, 31, bit_step, jnp.zeros((e, 1), I32))
    gt = aff >= pltpu.bitcast(tau + 1, F32)
    eq = (aff >= pltpu.bitcast(tau, F32)) & jnp.logical_not(gt)
    need = capf - jnp.sum(jnp.where(gt, 1.0, 0.0), axis=1, keepdims=True)

    r = lax.broadcasted_iota(I32, (LANES, LANES), 0)
    c = lax.broadcasted_iota(I32, (LANES, LANES), 1)
    upper = jnp.where(r < c, 1.0, 0.0).astype(BF16)
    eye = jnp.where(r == c, 1.0, 0.0).astype(BF16)

    def excl_prefix(mask_f):
        off = jnp.zeros((e, 1), F32)
        outs, starts = [], []
        for tb in range(nt):
            blk = mask_f[:, tb * LANES:(tb + 1) * LANES]
            outs.append(_dot(blk.astype(BF16), upper) + off)
            starts.append(off)
            off = off + jnp.sum(blk, axis=1, keepdims=True)
        return jnp.concatenate(outs, axis=1), starts

    eq_rank, _ = excl_prefix(jnp.where(eq, 1.0, 0.0))
    sel = gt | (eq & (eq_rank < need))
    pos, starts = excl_prefix(jnp.where(sel, 1.0, 0.0))
    selpos = jnp.where(sel, pos, -1.0)
    selpos_ref[0] = selpos.astype(I32)

    gsel_ref[0] = jnp.where(sel, aff, 0.0)

    lane = lax.broadcasted_iota(I32, (e, nt), 1)
    st = jnp.zeros((e, nt), F32)
    most = jnp.zeros((e, 1), F32)
    for tb in range(nt):
        st = jnp.where(lane == tb, starts[tb], st)
        stv_ref[0, tb] = starts[tb].astype(I32)
        nxt = starts[tb + 1] if tb + 1 < nt else capf
        most = jnp.maximum(most, nxt - starts[tb])
    st_ref[0] = st.astype(I32)
    most_ref[0] = jnp.max(most, axis=0, keepdims=True).astype(I32)

    code = selpos + 1.0
    chi = jnp.floor(code * (1.0 / 16.0))
    clo = code - 16.0 * chi
    for tb in range(nt):
        cols = slice(tb * LANES, (tb + 1) * LANES)
        tr = lambda a: _nt(eye, a[:, cols].astype(BF16))
        pm_ref[0, cols, :] = (16.0 * tr(chi) + tr(clo) - 1.0).astype(I32)


def _topk(aff_t, cap):
    b, e, n = aff_t.shape
    nt = n // LANES
    return pl.pallas_call(
        functools.partial(_topk_kernel, cap=cap),
        grid=(b,),
        in_specs=[pl.BlockSpec((1, e, n), lambda i: (i, 0, 0))],
        out_specs=[pl.BlockSpec((1, e, n), lambda i: (i, 0, 0)),
                   pl.BlockSpec((1, e, n), lambda i: (i, 0, 0)),
                   pl.BlockSpec((1, e, nt), lambda i: (i, 0, 0)),
                   pl.BlockSpec((1, nt, e, 1), lambda i: (i, 0, 0, 0)),
                   pl.BlockSpec((1, 1, 1), lambda i: (i, 0, 0)),
                   pl.BlockSpec((1, n, e), lambda i: (i, 0, 0))],
        out_shape=[jax.ShapeDtypeStruct((b, e, n), I32),
                   jax.ShapeDtypeStruct((b, e, n), F32),
                   jax.ShapeDtypeStruct((b, e, nt), I32),
                   jax.ShapeDtypeStruct((b, nt, e, 1), I32),
                   jax.ShapeDtypeStruct((b, 1, 1), I32),
                   jax.ShapeDtypeStruct((b, n, e), I32)],
        name="moe_topk",
        compiler_params=_cparams("arbitrary"),
    )(aff_t)


def _gather_kernel(st_ref, most_ref, selpos_ref, gsel_ref, stv_ref, h_ref, xs_ref, gs_ref, *, cap):
    bi = pl.program_id(0)
    ne = xs_ref.shape[0]
    nt = h_ref.shape[1] // LANES
    first_cols = pl.program_id(1) == 0
    xs_ref[...] = jnp.zeros_like(xs_ref)

    @pl.when(first_cols)
    def _():
        gs_ref[...] = jnp.zeros_like(gs_ref)

    keep_gate = jnp.where(first_cols, 1.0, 0.0)

    def run(win, unroll):
        riota = lax.broadcasted_iota(I32, (ne, win, LANES), 1)

        def body(tb, _):
            rows = pl.ds(pl.multiple_of(tb * LANES, LANES), LANES)
            slot = selpos_ref[0, :, pl.ds(tb, 1), :]
            p0v = jnp.minimum(jnp.bitwise_and(stv_ref[0, tb], -BF16_ROWS), cap - win)
            onehot = jnp.where(slot - p0v[:, :, None] == riota, 1.0, 0.0)
            res = _dot(onehot.reshape(ne * win, LANES).astype(BF16), h_ref[0, rows, :])
            gate = jnp.sum(onehot * gsel_ref[0, :, pl.ds(tb, 1), :], axis=-1, keepdims=True) * keep_gate
            for ei in range(ne):
                start = st_ref[(bi * ne + ei) * nt + tb]
                p0 = pl.multiple_of(jnp.minimum(jnp.bitwise_and(start, -BF16_ROWS), cap - win), BF16_ROWS)
                xs_ref[ei, 0, pl.ds(p0, win), :] += res[ei * win:(ei + 1) * win].astype(xs_ref.dtype)
                gs_ref[ei, 0, pl.ds(p0, win), :] += gate[ei]
            return 0

        lax.fori_loop(0, nt, body, 0, unroll=unroll)

    fast = min(GATHER_FAST_WIN, cap)
    full = min(GATHER_WIN, cap)
    most = most_ref[bi]

    @pl.when(most <= fast - BF16_ROWS)
    def _():
        run(fast, min(2, nt))

    @pl.when(most > fast - BF16_ROWS)
    def _():
        run(full, 1)


def _gather(st, most, selpos, gsel, stv, h2, cap):
    b, n, d = h2.shape
    e = selpos.shape[1]
    nt = n // LANES
    dh = d // 2
    grid_spec = pltpu.PrefetchScalarGridSpec(
        num_scalar_prefetch=2,
        grid=(b, d // dh),
        in_specs=[pl.BlockSpec((1, e, nt, LANES), lambda i, k, p, q: (i, 0, 0, 0)),
                  pl.BlockSpec((1, e, nt, LANES), lambda i, k, p, q: (i, 0, 0, 0)),
                  pl.BlockSpec((1, nt, e, 1), lambda i, k, p, q: (i, 0, 0, 0)),
                  pl.BlockSpec((1, n, dh), lambda i, k, p, q: (i, 0, k))],
        out_specs=[pl.BlockSpec((e, 1, cap, dh), lambda i, k, p, q: (0, i, 0, k)),
                   pl.BlockSpec((e, 1, cap, 1), lambda i, k, p, q: (0, i, 0, 0))],
    )
    return pl.pallas_call(
        functools.partial(_gather_kernel, cap=cap),
        grid_spec=grid_spec,
        out_shape=[jax.ShapeDtypeStruct((e, b, cap, d), BF16),
                   jax.ShapeDtypeStruct((e, b, cap, 1), F32)],
        name="moe_gather",
        compiler_params=_cparams("arbitrary", "arbitrary"),
    )(st.reshape(-1), most.reshape(-1), selpos.reshape(b, e, nt, LANES), gsel.reshape(b, e, nt, LANES), stv, h2)


def _ffn_kernel(*refs, ngroups):
    xs = refs[:ngroups]
    gs = refs[ngroups:2 * ngroups]
    wg_ref, wu_ref, wd_ref = refs[2 * ngroups:2 * ngroups + 3]
    ys = refs[2 * ngroups + 3:3 * ngroups + 3]
    accs = refs[3 * ngroups + 3:]
    f = pl.program_id(1)
    wg = wg_ref[0, 0].astype(BF16)
    wu = wu_ref[0, 0].astype(BF16)
    wd = wd_ref[0, 0].astype(BF16)
    for x_ref, g_ref, y_ref, acc_ref in zip(xs, gs, ys, accs):
        rows = x_ref.shape[1]
        tr = min(512, rows)

        @pl.when(f == 0)
        def _(acc_ref=acc_ref):
            acc_ref[...] = jnp.zeros_like(acc_ref)

        def body(i, _, x_ref=x_ref, acc_ref=acc_ref, tr=tr):
            sl = pl.ds(pl.multiple_of(i * tr, tr), tr)
            x = x_ref[0, sl, :]
            g = _dot(x, wg)
            u = _dot(x, wu)
            acc_ref[sl, :] += _dot((g * _sigmoid(g) * u).astype(BF16), wd)
            return 0

        lax.fori_loop(0, rows // tr, body, 0)

        @pl.when(f == pl.num_programs(1) - 1)
        def _(y_ref=y_ref, acc_ref=acc_ref, g_ref=g_ref):
            y_ref[0] = (acc_ref[...] * g_ref[0]).astype(y_ref.dtype)


def _ffn(xs_list, gs_list, wg, wu, wd, layer):
    _, e, d, ff = wg.shape
    tf = min(512, ff)
    ng = len(xs_list)
    in_specs = [pl.BlockSpec((1, x.shape[1], d), lambda i, j: (i, 0, 0)) for x in xs_list]
    in_specs += [pl.BlockSpec((1, x.shape[1], 1), lambda i, j: (i, 0, 0)) for x in xs_list]
    in_specs += [pl.BlockSpec((1, 1, d, tf), lambda i, j: (layer, i, 0, j)),
                 pl.BlockSpec((1, 1, d, tf), lambda i, j: (layer, i, 0, j)),
                 pl.BlockSpec((1, 1, tf, d), lambda i, j: (layer, i, j, 0))]
    return pl.pallas_call(
        functools.partial(_ffn_kernel, ngroups=ng),
        grid=(e, ff // tf),
        in_specs=in_specs,
        out_specs=[pl.BlockSpec((1, x.shape[1], d), lambda i, j: (i, 0, 0)) for x in xs_list],
        out_shape=[jax.ShapeDtypeStruct(x.shape, BF16) for x in xs_list],
        scratch_shapes=[pltpu.VMEM((x.shape[1], d), F32) for x in xs_list],
        name="moe_ffn",
        compiler_params=_cparams("arbitrary", "arbitrary"),
    )(*xs_list, *gs_list, wg, wu, wd)


def _combine_kernel(st_ref, most_ref, pm_ref, y_ref, x_ref, mod_ref, o_ref, ystk_ref, *, cap, nt_total):
    bi = pl.program_id(0)
    tg = pl.program_id(2)
    ne = y_ref.shape[0]
    tm = x_ref.shape[1]
    mgate = mod_ref[0][5:6]
    fwin = min(SCATTER_FAST_WIN, cap)
    per = LANES // fwin
    swin = min(SCATTER_WIN, cap)

    def window_start(ei, tb, win):
        start = st_ref[(bi * ne + ei) * nt_total + tb]
        return pl.multiple_of(jnp.minimum(jnp.bitwise_and(start, -BF16_ROWS), cap - win), BF16_ROWS)

    def fast_body(t, _):
        tb = tg * (tm // LANES) + t
        rows = pl.ds(pl.multiple_of(t * LANES, LANES), LANES)
        pm = pm_ref[0, rows, :]
        lane = lax.broadcasted_iota(I32, (LANES, LANES), 1)
        sub = jnp.bitwise_and(lane, fwin - 1)
        grp = jnp.right_shift(lane, fwin.bit_length() - 1)
        pieces = []
        for j in range(ne // per):
            rel = None
            for u in range(per):
                ei = j * per + u
                p0 = window_start(ei, tb, fwin)
                ystk_ref[ei * fwin:(ei + 1) * fwin, :] = y_ref[ei, 0, pl.ds(p0, fwin), :]
                r = pm[:, ei:ei + 1] - p0
                rel = jnp.broadcast_to(r, (LANES, LANES)) if u == 0 else jnp.where(grp == u, r, rel)
            pieces.append(jnp.where(rel == sub, 1.0, 0.0).astype(BF16))
        acc = _dot(jnp.concatenate(pieces, axis=1), ystk_ref[...])
        o_ref[0, rows, :] = x_ref[0, rows, :] + mgate * acc
        return 0

    def full_body(t, _):
        tb = tg * (tm // LANES) + t
        rows = pl.ds(pl.multiple_of(t * LANES, LANES), LANES)
        pm = pm_ref[0, rows, :]
        liota = lax.broadcasted_iota(I32, (LANES, swin), 1)
        acc = jnp.zeros((LANES, x_ref.shape[2]), F32)
        for ei in range(ne):
            p0 = window_start(ei, tb, swin)
            onehot = jnp.where(pm[:, ei:ei + 1] - p0 == liota, 1.0, 0.0).astype(BF16)
            acc = acc + _dot(onehot, y_ref[ei, 0, pl.ds(p0, swin), :])
        o_ref[0, rows, :] = x_ref[0, rows, :] + mgate * acc
        return 0

    most = most_ref[bi]

    @pl.when(most <= fwin - BF16_ROWS)
    def _():
        lax.fori_loop(0, tm // LANES, fast_body, 0)

    @pl.when(most > fwin - BF16_ROWS)
    def _():
        lax.fori_loop(0, tm // LANES, full_body, 0)


def _combine(st, most, pm, y, x, mod, cap):
    b, n, d = x.shape
    e = y.shape[0]
    tm = min(512, n)
    dh = d // 2
    nt = n // LANES
    grid_spec = pltpu.PrefetchScalarGridSpec(
        num_scalar_prefetch=2,
        grid=(b, d // dh, n // tm),
        in_specs=[pl.BlockSpec((1, tm, e), lambda i, k, j, p, q: (i, j, 0)),
                  pl.BlockSpec((e, 1, cap, dh), lambda i, k, j, p, q: (0, i, 0, k)),
                  pl.BlockSpec((1, tm, dh), lambda i, k, j, p, q: (i, j, k)),
                  pl.BlockSpec((1, 6, dh), lambda i, k, j, p, q: (i, 0, k))],
        out_specs=pl.BlockSpec((1, tm, dh), lambda i, k, j, p, q: (i, j, k)),
        scratch_shapes=[pltpu.VMEM((e * min(SCATTER_FAST_WIN, cap), dh), BF16)],
    )
    return pl.pallas_call(
        functools.partial(_combine_kernel, cap=cap, nt_total=nt),
        grid_spec=grid_spec,
        out_shape=jax.ShapeDtypeStruct((b, n, d), F32),
        name="moe_combine",
        compiler_params=_cparams("arbitrary", "arbitrary", "arbitrary"),
    )(st.reshape(-1), most.reshape(-1), pm, y, x, mod)


def _final_kernel(x_ref, g_ref, o_ref):
    o_ref[0] = _rms(x_ref[0], g_ref[...])


def _final_norm(x, g):
    b, n, d = x.shape
    tm = min(1024, n)
    return pl.pallas_call(
        _final_kernel,
        grid=(b, n // tm),
        in_specs=[pl.BlockSpec((1, tm, d), lambda i, j: (i, j, 0)),
                  pl.BlockSpec((1, d), lambda i, j: (0, 0))],
        out_specs=pl.BlockSpec((1, tm, d), lambda i, j: (i, j, 0)),
        out_shape=jax.ShapeDtypeStruct((b, n, d), F32),
        name="final_norm",
        compiler_params=_cparams("arbitrary", "arbitrary"),
    )(x, g)


def _rot_cols(w):
    q = MLA_ROPE // 4
    return jnp.concatenate([-w[:, q:2 * q], w[:, :q], -w[:, 3 * q:], w[:, 2 * q:3 * q]], axis=1)


def _prep_even(w_in, wq_b, wkv_b):
    kpe = w_in[:, -MLA_ROPE:]
    win = jnp.concatenate([w_in, _rot_cols(kpe)], axis=1).astype(BF16)
    qh = wq_b.reshape(MLA_Q_RANK, MLA_HEADS, MLA_NOPE + MLA_ROPE)
    nope = [qh[:, h, :MLA_NOPE] for h in range(MLA_HEADS)]
    pe = [jnp.concatenate([qh[:, h, MLA_NOPE:], _rot_cols(qh[:, h, MLA_NOPE:])], axis=1) for h in range(MLA_HEADS)]
    wq = jnp.concatenate(nope + pe, axis=1).astype(BF16)
    kvh = wkv_b.reshape(MLA_KV_RANK, MLA_HEADS, MLA_NOPE + MLA_V)
    wkv = jnp.concatenate([kvh[:, h, :MLA_NOPE] for h in range(MLA_HEADS)]
                          + [kvh[:, h, MLA_NOPE:] for h in range(MLA_HEADS)], axis=1).astype(BF16)
    return win, wq, wkv


def _rope_tables(n):
    rows = n // GRID_W
    row = jnp.repeat(jnp.arange(rows, dtype=F32), GRID_W)
    col = jnp.tile(jnp.arange(GRID_W, dtype=F32), rows)
    half = MLA_ROPE // 2
    inv = 1.0 / (ROPE_THETA ** (jnp.arange(0, half, 2, dtype=F32) / half))
    ar = row[:, None] * inv[None, :]
    ac = col[:, None] * inv[None, :]
    ang = jnp.concatenate([ar, ar, ac, ac], axis=-1)
    z = jnp.zeros((n, LANES - MLA_ROPE), F32)
    return jnp.concatenate([jnp.cos(ang), z], axis=1), jnp.concatenate([jnp.sin(ang), z], axis=1)


def _moe(x, mod, g2, rt, wg, wu, wd, layer, xc=None, modc=None):
    streams = [(x, mod)] + ([(xc, modc)] if xc is not None else [])
    routed = []
    for xx, mm in streams:
        b, n, d = xx.shape
        e = rt.shape[0]
        cap = EC_FACTOR * n // e
        h2, aff_t = _route(xx, mm, g2, rt)
        selpos, gsel, st, stv, most, pm = _topk(aff_t, cap)
        xs, gs = _gather(st, most, selpos, gsel, stv, h2, cap)
        routed.append((st, most, pm, cap, xs, gs))
    flat = lambda a: a.reshape(a.shape[0], -1, a.shape[3])
    ys = _ffn([flat(r[4]) for r in routed], [flat(r[5]) for r in routed], wg, wu, wd, layer)
    outs = []
    for (xx, mm), (st, most, pm, cap, xs, gs), y in zip(streams, routed, ys):
        outs.append(_combine(st, most, pm, y.reshape(xs.shape), xx, mm, cap))
    return outs if xc is not None else outs[0]


def kernel(x, c, ctx, c_ctx, ada_w, ada_b, norm1_g, norm2_g, w_in, hg_lb, hg_norm_g, mla_qn_g, mla_wq_b,
           mla_kvn_g, mla_wkv_b, w_out, pool_w, pool_scale, router_w, exp_wg, exp_wu, exp_wd, final_g):
    b, n, d = x.shape
    nctx = ctx.shape[1]
    depth = ada_w.shape[0]
    n_even = w_in.shape[0]
    assert b + 1 <= 8 and n % LANES == 0 and nctx % LANES == 0 and n % GRID_W == 0

    cc = jnp.zeros((8, d), F32).at[:b].set(c).at[b].set(c_ctx)
    mods = _ada(cc, ada_w, ada_b)
    mod_lat = [mods[l, :b].reshape(b, 6, d) for l in range(depth)]
    mod_ctx = [jnp.broadcast_to(mods[l, b].reshape(1, 6, d), (b, 6, d)) for l in range(depth)]

    lb_all = jnp.cumsum(jax.nn.softmax(hg_lb.astype(F32), axis=0), axis=0)
    lb_all = lb_all - lb_all[:1]
    cos1, sin1 = _rope_tables(n)
    cos_c = jnp.concatenate([jnp.ones((nctx, MLA_ROPE), F32), jnp.zeros((nctx, LANES - MLA_ROPE), F32)], axis=1)
    sin_c = jnp.zeros((nctx, LANES), F32)

    last_reader = 2 * (n_even - 1)
    row = lambda v: v.reshape(1, -1)
    x_lat, x_ctx = x, ctx
    for l in range(depth):
        j = l // 2
        ctx_in = l <= last_reader
        ctx_out = l < last_reader
        if l % 2 == 0:
            win, wq, wkv = _prep_even(w_in[j], mla_wq_b[j], mla_wkv_b[j])
            lb = lb_all[j]
            lbp = jnp.stack([jnp.log(lb[0]), jnp.log1p(-lb[0]), 1.0 - lb[0],
                             jnp.log(lb[1]), jnp.log1p(-lb[1]), 1.0 - lb[1]])
            common = (row(norm1_g[l]), win, row(mla_qn_g[j]), wq, row(mla_kvn_g[j]), wkv)
            hg_l, q_l, k_l, v_l = _in_proj(x_lat, mod_lat[l], *common, cos1, sin1)
            hg_c = None
            ks, vs = [k_l], [v_l]
            if ctx_in:
                hg_c, q_c, k_c, v_c = _in_proj(x_ctx, mod_ctx[l], *common, cos_c, sin_c)
                ks, vs = [k_c, k_l], [v_c, v_l]
            hgo_l, hgo_c = _hgrn(hg_l, hg_c, lbp, row(hg_norm_g[j]), ctx_out)
            mla_l = _attention(q_l, ks, vs)
            wo = w_out[j].astype(BF16)
            x_lat = _out_proj(hgo_l, mla_l, x_lat, mod_lat[l], wo)
            if ctx_out:
                mla_c = _attention(q_c, [k_c], [v_c])
                x_ctx = _out_proj(hgo_c, mla_c, x_ctx, mod_ctx[l], wo)
        else:
            pw = pool_w[j].astype(BF16)
            x_lat = _pool(x_lat, mod_lat[l], row(norm1_g[l]), pw, row(pool_scale[j]))
            if ctx_out:
                x_ctx = _pool(x_ctx, mod_ctx[l], row(norm1_g[l]), pw, row(pool_scale[j]))
        rt = router_w[l].T
        if ctx_out:
            x_lat, x_ctx = _moe(x_lat, mod_lat[l], row(norm2_g[l]), rt, exp_wg, exp_wu, exp_wd, l,
                                x_ctx, mod_ctx[l])
        else:
            x_lat = _moe(x_lat, mod_lat[l], row(norm2_g[l]), rt, exp_wg, exp_wu, exp_wd, l)
    return _final_norm(x_lat, row(final_g))
```

```python
import functools

import numpy as np
import jax
import jax.numpy as jnp
from jax import lax
from jax.experimental import pallas as pl
from jax.experimental.pallas import tpu as pltpu

F32 = jnp.float32
BF16 = jnp.bfloat16
I32 = jnp.int32

EPS = 1e-6
GRID_W = 64
ROPE_THETA = 10000.0
HG_HEADS = 4
HG_D = 128
MLA_HEADS = 4
MLA_Q_RANK = 256
MLA_KV_RANK = 128
MLA_NOPE = 128
MLA_ROPE = 64
MLA_V = 128
MLA_SCALE = (MLA_NOPE + MLA_ROPE) ** -0.5
POOL_WINDOWS = (2, 4, 8, 16)
EC_FACTOR = 2
HG_W = HG_HEADS * HG_D

LANES = 128
CHUNK = 64
POOL_HALO = 8
SUBLANES = 8
BF16_ROWS = 16
GATHER_WIN = LANES + BF16_ROWS
GATHER_FAST_WIN = 48
SCATTER_WIN = 256
SCATTER_FAST_WIN = 64
ATTN_Q_SUB = 512
LOG2E = 1.4426950408889634
VMEM_LIMIT_V7X = 56 * 1024 * 1024


def _cparams(*sem):
    return pltpu.CompilerParams(dimension_semantics=sem, vmem_limit_bytes=VMEM_LIMIT_V7X)


def _nt(a, b):
    return lax.dot_general(a, b, (((1,), (1,)), ((), ())), preferred_element_type=F32)


def _tn(a, b):
    return lax.dot_general(a, b, (((0,), (0,)), ((), ())), preferred_element_type=F32)


def _dot(a, b):
    return jnp.dot(a, b, preferred_element_type=F32)


def _sigmoid(x):
    return 1.0 / (1.0 + jnp.exp(-x))


def _rms(x, g):
    return x * lax.rsqrt(jnp.mean(x * x, axis=-1, keepdims=True) + EPS) * g


def _modulate(x, g, shift, scale):
    return _rms(x, g) * (1.0 + scale) + shift


def _split3(x):
    hi = x.astype(BF16)
    r1 = x - hi.astype(F32)
    mid = r1.astype(BF16)
    lo = (r1 - mid.astype(F32)).astype(BF16)
    return hi, mid, lo


def _ada_kernel(c_ref, w_ref, b_ref, o_ref):
    c = c_ref[...]
    s = (c * _sigmoid(c)).astype(BF16)
    o_ref[0] = _dot(s, w_ref[0].astype(BF16)) + b_ref[0]


def _ada(cc, ada_w, ada_b):
    depth, d, d6 = ada_w.shape
    tn = d6 // 4
    return pl.pallas_call(
        _ada_kernel,
        grid=(depth, d6 // tn),
        in_specs=[pl.BlockSpec((8, d), lambda l, j: (0, 0)),
                  pl.BlockSpec((1, d, tn), lambda l, j: (l, 0, j)),
                  pl.BlockSpec((1, 1, tn), lambda l, j: (l, 0, j))],
        out_specs=pl.BlockSpec((1, 8, tn), lambda l, j: (l, 0, j)),
        out_shape=jax.ShapeDtypeStruct((depth, 8, d6), F32),
        name="ada_mod",
        compiler_params=_cparams("arbitrary", "arbitrary"),
    )(cc, ada_w, ada_b.reshape(depth, 1, d6))


def _in_kernel(x_ref, mod_ref, g1_ref, win_ref, qng_ref, wq_ref, kvg_ref, wkv_ref, cs1_ref, cs2_ref,
               hg_ref, q_ref, k_ref, vt_ref):
    m = mod_ref[0]
    h = _modulate(x_ref[0], g1_ref[...], m[0:1], m[1:2]).astype(BF16)
    proj = _dot(h, win_ref[...])
    o = 5 * HG_W
    hg_ref[0] = proj[:, :o]
    qa = proj[:, o:o + MLA_Q_RANK]
    kva = proj[:, o + MLA_Q_RANK:o + MLA_Q_RANK + MLA_KV_RANK]
    kp = proj[:, o + MLA_Q_RANK + MLA_KV_RANK:]
    qup = _dot(_rms(qa, qng_ref[...]).astype(BF16), wq_ref[...])
    kvup = _dot(_rms(kva, kvg_ref[...]).astype(BF16), wkv_ref[...])
    cs1 = cs1_ref[...]
    cs2 = cs2_ref[...]
    half = LANES // 2
    kpe = (kp * cs1 + pltpu.roll(kp, half, axis=1) * cs2).astype(BF16)
    nn = MLA_HEADS * MLA_NOPE
    qs = MLA_SCALE * LOG2E
    for hd in range(MLA_HEADS):
        p = qup[:, nn + LANES * hd:nn + LANES * (hd + 1)]
        qpe = p * cs1 + pltpu.roll(p, half, axis=1) * cs2
        q_ref[0, :, 2 * LANES * hd:2 * LANES * hd + LANES] = (qup[:, LANES * hd:LANES * (hd + 1)] * qs).astype(BF16)
        q_ref[0, :, 2 * LANES * hd + LANES:2 * LANES * (hd + 1)] = (qpe * qs).astype(BF16)
        k_ref[0, :, 2 * LANES * hd:2 * LANES * hd + LANES] = kvup[:, LANES * hd:LANES * (hd + 1)].astype(BF16)
        k_ref[0, :, 2 * LANES * hd + LANES:2 * LANES * (hd + 1)] = kpe
    vt_ref[0] = kvup[:, nn:].T.astype(BF16)


def _in_proj(x, mod, g1, win, qng, wq, kvg, wkv, cs1, cs2):
    b, n, d = x.shape
    tm = min(256, n)
    wcols = win.shape[1]
    full = lambda shape: pl.BlockSpec(shape, lambda i, j: (0,) * len(shape))
    return pl.pallas_call(
        _in_kernel,
        grid=(b, n // tm),
        in_specs=[pl.BlockSpec((1, tm, d), lambda i, j: (i, j, 0)),
                  pl.BlockSpec((1, 6, d), lambda i, j: (i, 0, 0)),
                  full((1, d)), full((d, wcols)), full((1, MLA_Q_RANK)), full(wq.shape),
                  full((1, MLA_KV_RANK)), full(wkv.shape),
                  pl.BlockSpec((tm, LANES), lambda i, j: (j, 0)),
                  pl.BlockSpec((tm, LANES), lambda i, j: (j, 0))],
        out_specs=[pl.BlockSpec((1, tm, 5 * HG_W), lambda i, j: (i, j, 0)),
                   pl.BlockSpec((1, tm, 2 * LANES * MLA_HEADS), lambda i, j: (i, j, 0)),
                   pl.BlockSpec((1, tm, 2 * LANES * MLA_HEADS), lambda i, j: (i, j, 0)),
                   pl.BlockSpec((1, MLA_HEADS * MLA_V, tm), lambda i, j: (i, 0, j))],
        out_shape=[jax.ShapeDtypeStruct((b, n, 5 * HG_W), F32),
                   jax.ShapeDtypeStruct((b, n, 2 * LANES * MLA_HEADS), BF16),
                   jax.ShapeDtypeStruct((b, n, 2 * LANES * MLA_HEADS), BF16),
                   jax.ShapeDtypeStruct((b, MLA_HEADS * MLA_V, n), BF16)],
        name="in_proj",
        compiler_params=_cparams("arbitrary", "arbitrary"),
    )(x, mod, g1, win, qng, wq, kvg, wkv, cs1, cs2)


def _attn_kernel(*refs, nparts):
    q_ref = refs[0]
    k_refs = refs[1:1 + nparts]
    v_refs = refs[1 + nparts:1 + 2 * nparts]
    o_ref = refs[1 + 2 * nparts]
    tq = q_ref.shape[1]
    sub = min(ATTN_Q_SUB, tq)
    scores = [[_nt(k[0], q_ref[0, q0:q0 + sub, :]) for k in k_refs] for q0 in range(0, tq, sub)]
    for q0, s in zip(range(0, tq, sub), scores):
        m = functools.reduce(jnp.maximum, [jnp.max(x, axis=0, keepdims=True) for x in s])
        p = [jnp.exp2(x - m) for x in s]
        den = functools.reduce(jnp.add, [jnp.sum(x, axis=0, keepdims=True) for x in p])
        acc = functools.reduce(jnp.add, [_dot(vt[0], x.astype(BF16)) for vt, x in zip(v_refs, p)])
        o_ref[0, q0:q0 + sub, :] = (acc / den).T.astype(o_ref.dtype)


def _attention(q, ks, vts):
    b, n, _ = q.shape
    tq = min(2 * ATTN_Q_SUB, n)
    nparts = len(ks)
    in_specs = [pl.BlockSpec((1, tq, 2 * LANES), lambda i, h, j: (i, j, h))]
    in_specs += [pl.BlockSpec((1, k.shape[1], 2 * LANES), lambda i, h, j: (i, 0, h)) for k in ks]
    in_specs += [pl.BlockSpec((1, MLA_V, v.shape[2]), lambda i, h, j: (i, h, 0)) for v in vts]
    return pl.pallas_call(
        functools.partial(_attn_kernel, nparts=nparts),
        grid=(b, MLA_HEADS, n // tq),
        in_specs=in_specs,
        out_specs=pl.BlockSpec((1, tq, MLA_V), lambda i, h, j: (i, j, h)),
        out_shape=jax.ShapeDtypeStruct((b, n, MLA_HEADS * MLA_V), BF16),
        name="mla_attention",
        compiler_params=_cparams("arbitrary", "arbitrary", "arbitrary"),
    )(q, *ks, *vts)


_HG_HALVES = (1, 2, 4, 8, 16, 32)


def _hgrn_masks():
    t = np.arange(CHUNK)[:, None]
    s = np.arange(CHUNK)[None, :]
    masks_f = [(t == s)]
    for h in _HG_HALVES:
        masks_f.append((t // (2 * h) == s // (2 * h)) & (t % (2 * h) >= h) & (s % (2 * h) < h))
    masks_f = np.stack(masks_f).astype(np.float32)
    return masks_f, np.transpose(masks_f, (0, 2, 1))


def _hg_block_decays(f, fw):
    c = CHUNK
    pre, suf = [f], [jnp.ones_like(f)]
    h = 1
    while h < c:
        blk = 2 * h
        src_early = h - 1 if fw else h
        src_late = blk - 1 if fw else 0
        p = pre[-1]
        if blk <= SUBLANES:
            p3 = p.reshape(c // SUBLANES, SUBLANES, HG_D)
            r = lax.broadcasted_iota(I32, p3.shape, 1)
            tot_e = jnp.ones_like(p3)
            tot_l = jnp.ones_like(p3)
            for kb in range(SUBLANES // blk):
                inblk = jnp.right_shift(r, blk.bit_length() - 1) == kb
                tot_e = jnp.where(inblk, p3[:, kb * blk + src_early:kb * blk + src_early + 1, :], tot_e)
                tot_l = jnp.where(inblk, p3[:, kb * blk + src_late:kb * blk + src_late + 1, :], tot_l)
        else:
            p3 = p.reshape(c // blk, blk, HG_D)
            r = lax.broadcasted_iota(I32, p3.shape, 1)
            tot_e = jnp.broadcast_to(p3[:, src_early:src_early + 1, :], p3.shape)
            tot_l = jnp.broadcast_to(p3[:, src_late:src_late + 1, :], p3.shape)
        in_first = jnp.bitwise_and(r, blk - 1) < h
        late = jnp.logical_not(in_first) if fw else in_first
        s3 = suf[-1].reshape(p3.shape)
        pre.append(jnp.where(late, p3 * tot_e, p3).reshape(c, HG_D))
        suf.append(jnp.where(late, s3, s3 * tot_l).reshape(c, HG_D))
        h = blk
    return pre, suf


def _hg_chunk(q, z, v, lb, onemlb, masks, st, fw, want_out):
    t = jnp.exp(-jnp.abs(z))
    r = 1.0 / (1.0 + t)
    pos = z >= 0.0
    f = lb + onemlb * jnp.where(pos, r, t * r)
    k = onemlb * jnp.where(pos, t * r, r)
    pre, suf = _hg_block_decays(f, fw)
    vb = v.astype(BF16)
    whole = pre[-1]
    kdl = (k * suf[-1]).astype(BF16)
    last = CHUNK - 1 if fw else 0
    dec = whole[last:last + 1]
    out = None
    if want_out:
        kb = k.astype(BF16)
        att = masks[0] * _nt(q.astype(BF16), kb)
        for i in range(len(_HG_HALVES)):
            a = (q * pre[i]).astype(BF16)
            bm = kb if i == 0 else (k * suf[i]).astype(BF16)
            att = att + masks[1 + i] * _nt(a, bm)
        out = _dot(att.astype(BF16), vb) + _nt((q * whole).astype(BF16), st.astype(BF16))
    st_new = st * dec + _tn(vb, kdl)
    return out, st_new


def _hgrn_kernel(*refs, has_ctx, ctx_out):
    it = iter(refs)
    q_ref, zf_ref, zb_ref, v_ref, g_ref = (next(it) for _ in range(5))
    if has_ctx:
        qc_ref, zfc_ref, zbc_ref, vc_ref, gc_ref = (next(it) for _ in range(5))
    lb_ref, gn_ref, mkf_ref, mkb_ref = (next(it) for _ in range(4))
    o_ref = next(it)
    oc_ref = next(it) if ctx_out else None
    of_s, ob_s = next(it), next(it)
    if ctx_out:
        ocf_s, ocb_s = next(it), next(it)

    lbp = lb_ref[...]
    masks = (mkf_ref[...], mkb_ref[...])
    zrefs = (zf_ref, zb_ref)

    def seq_pass(q_r, z_rs, v_r, o_ss, n, states, want_out):
        nc = n // CHUNK

        def body(ci, carry):
            new = []
            for d in range(2):
                cidx = ci if d == 0 else nc - 1 - ci
                off = pl.multiple_of(cidx * CHUNK, CHUNK)
                rows = pl.ds(off, CHUNK)
                qv = q_r[0, rows, :] if want_out else None
                out, st = _hg_chunk(qv, z_rs[d][0, rows, :], v_r[0, rows, :],
                                    lbp[2 * d:2 * d + 1], lbp[2 * d + 1:2 * d + 2],
                                    masks[d], carry[d], d == 0, want_out)
                if want_out:
                    o_ss[d][rows, :] = out
                new.append(st)
            return tuple(new)

        return lax.fori_loop(0, nc, body, states, unroll=4)

    def readout(o_ss, g_r, out_r, n):
        tr = min(512, n)

        def body(i, _):
            rows = pl.ds(pl.multiple_of(i * tr, tr), tr)
            o = o_ss[0][rows, :] + o_ss[1][rows, :]
            g = g_r[0, rows, :]
            out_r[0, rows, :] = (_rms(o, gn_ref[...]) * (g * _sigmoid(g))).astype(out_r.dtype)
            return 0

        lax.fori_loop(0, n // tr, body, 0)

    zero = jnp.zeros((HG_D, HG_D), F32)
    states = (zero, zero)
    if has_ctx:
        nctx = vc_ref.shape[1]
        states = seq_pass(qc_ref, (zfc_ref, zbc_ref), vc_ref, (ocf_s, ocb_s) if ctx_out else None,
                          nctx, states, ctx_out)
        if ctx_out:
            readout((ocf_s, ocb_s), gc_ref, oc_ref, nctx)
    n = v_ref.shape[1]
    seq_pass(q_ref, zrefs, v_ref, (of_s, ob_s), n, states, True)
    readout((of_s, ob_s), g_ref, o_ref, n)


def _hgrn(hg, hg_ctx, lbp, gn, ctx_out):
    b, n, _ = hg.shape
    has_ctx = hg_ctx is not None
    consts = [jnp.asarray(a, F32) for a in _hgrn_masks()]

    def part(nn, k):
        return pl.BlockSpec((1, nn, HG_D), lambda i, h, k=k: (i, 0, k * HG_HEADS + h))

    in_specs = [part(n, k) for k in range(5)]
    args = [hg] * 5
    if has_ctx:
        nctx = hg_ctx.shape[1]
        in_specs += [part(nctx, k) for k in range(5)]
        args += [hg_ctx] * 5
    in_specs += [pl.BlockSpec((4, HG_D), lambda i, h: (0, h)),
                 pl.BlockSpec((1, HG_D), lambda i, h: (0, 0))]
    in_specs += [pl.BlockSpec(a.shape, lambda i, h, nd=a.ndim: (0,) * nd) for a in consts]
    args += [lbp, gn] + consts
    out_specs = [pl.BlockSpec((1, n, HG_D), lambda i, h: (i, 0, h))]
    out_shape = [jax.ShapeDtypeStruct((b, n, HG_W), BF16)]
    scratch = [pltpu.VMEM((n, HG_D), F32), pltpu.VMEM((n, HG_D), F32)]
    if ctx_out:
        out_specs.append(pl.BlockSpec((1, nctx, HG_D), lambda i, h: (i, 0, h)))
        out_shape.append(jax.ShapeDtypeStruct((b, nctx, HG_W), BF16))
        scratch += [pltpu.VMEM((nctx, HG_D), F32), pltpu.VMEM((nctx, HG_D), F32)]
    res = pl.pallas_call(
        functools.partial(_hgrn_kernel, has_ctx=has_ctx, ctx_out=ctx_out),
        grid=(b, HG_HEADS),
        in_specs=in_specs, out_specs=out_specs, out_shape=out_shape, scratch_shapes=scratch,
        name="hgrn_scan",
        compiler_params=_cparams("arbitrary", "arbitrary"),
    )(*args)
    return (res[0], res[1]) if ctx_out else (res[0], None)


def _out_kernel(hg_ref, mla_ref, x_ref, mod_ref, w_ref, o_ref):
    w = w_ref[...]
    y = _dot(hg_ref[0], w[:HG_W]) + _dot(mla_ref[0], w[HG_W:])
    o_ref[0] = x_ref[0] + mod_ref[0][2:3] * y


def _out_proj(hgo, mla, x, mod, w):
    b, n, d = x.shape
    tm = min(512, n)
    return pl.pallas_call(
        _out_kernel,
        grid=(b, n // tm),
        in_specs=[pl.BlockSpec((1, tm, HG_W), lambda i, j: (i, j, 0)),
                  pl.BlockSpec((1, tm, MLA_HEADS * MLA_V), lambda i, j: (i, j, 0)),
                  pl.BlockSpec((1, tm, d), lambda i, j: (i, j, 0)),
                  pl.BlockSpec((1, 6, d), lambda i, j: (i, 0, 0)),
                  pl.BlockSpec(w.shape, lambda i, j: (0, 0))],
        out_specs=pl.BlockSpec((1, tm, d), lambda i, j: (i, j, 0)),
        out_shape=jax.ShapeDtypeStruct((b, n, d), F32),
        name="out_proj",
        compiler_params=_cparams("arbitrary", "arbitrary"),
    )(hgo, mla, x, mod, w)


def _pool_kernel(x_ref, prev_ref, next_ref, mod_ref, g1_ref, pw_ref, ps_ref, o_ref, hs_ref, *, n):
    tm = x_ref.shape[1]
    d = x_ref.shape[2]
    gw = d // len(POOL_WINDOWS)
    m = mod_ref[0]
    base = pl.program_id(1) * tm
    xe = jnp.concatenate([prev_ref[0], x_ref[0], next_ref[0]], axis=0)
    he = _modulate(xe, g1_ref[...], m[0:1], m[1:2])
    gidx = base - POOL_HALO + lax.broadcasted_iota(I32, (tm + 2 * POOL_HALO, 1), 0)
    hs_ref[...] = jnp.where((gidx >= 0) & (gidx < n), he, 0.0)
    t = base + lax.broadcasted_iota(I32, (tm, 1), 0)
    ps = ps_ref[...]
    for gi, w in enumerate(POOL_WINDOWS):
        cols = slice(gi * gw, (gi + 1) * gw)
        acc = hs_ref[pl.ds(POOL_HALO - w // 2, tm), cols]
        for j in range(1 - w // 2, w // 2):
            acc = acc + hs_ref[pl.ds(POOL_HALO + j, tm), cols]
        lo = jnp.clip(t - w // 2, 0, n - 1)
        hi = jnp.clip(t + w // 2 - 1, 0, n - 1)
        pooled = acc / (hi - lo + 1).astype(F32) - hs_ref[pl.ds(POOL_HALO, tm), cols]
        y = _dot(pooled.astype(BF16), pw_ref[gi]) * ps[:, cols]
        o_ref[0, :, cols] = x_ref[0, :, cols] + m[2:3, cols] * y


def _pool(x, mod, g1, pw, ps):
    b, n, d = x.shape
    tm = min(512, n)
    nb = tm // POOL_HALO
    last = n // POOL_HALO - 1
    return pl.pallas_call(
        functools.partial(_pool_kernel, n=n),
        grid=(b, n // tm),
        in_specs=[pl.BlockSpec((1, tm, d), lambda i, j: (i, j, 0)),
                  pl.BlockSpec((1, POOL_HALO, d), lambda i, j: (i, jnp.maximum(j * nb - 1, 0), 0)),
                  pl.BlockSpec((1, POOL_HALO, d), lambda i, j: (i, jnp.minimum((j + 1) * nb, last), 0)),
                  pl.BlockSpec((1, 6, d), lambda i, j: (i, 0, 0)),
                  pl.BlockSpec((1, d), lambda i, j: (0, 0)),
                  pl.BlockSpec(pw.shape, lambda i, j: (0, 0, 0)),
                  pl.BlockSpec((1, d), lambda i, j: (0, 0))],
        out_specs=pl.BlockSpec((1, tm, d), lambda i, j: (i, j, 0)),
        out_shape=jax.ShapeDtypeStruct((b, n, d), F32),
        scratch_shapes=[pltpu.VMEM((tm + 2 * POOL_HALO, d), F32)],
        name="pool_mixer",
        compiler_params=_cparams("arbitrary", "arbitrary"),
    )(x, x, x, mod, g1, pw, ps)


def _route_kernel(x_ref, mod_ref, g2_ref, rt_ref, h_ref, aff_ref):
    m = mod_ref[0]
    h = _modulate(x_ref[0], g2_ref[...], m[3:4], m[4:5])
    hb = h.astype(BF16)
    h_ref[0] = hb
    hlo = (h - hb.astype(F32)).astype(BF16)
    rt = rt_ref[...]
    rhi = rt.astype(BF16)
    rlo = (rt - rhi.astype(F32)).astype(BF16)
    logits = _nt(rhi, hb) + (_nt(rhi, hlo) + _nt(rlo, hb))
    ex = jnp.exp(logits - jnp.max(logits, axis=0, keepdims=True))
    aff_ref[0] = ex / jnp.sum(ex, axis=0, keepdims=True)


def _route(x, mod, g2, rt):
    b, n, d = x.shape
    e = rt.shape[0]
    tm = min(512, n)
    return pl.pallas_call(
        _route_kernel,
        grid=(b, n // tm),
        in_specs=[pl.BlockSpec((1, tm, d), lambda i, j: (i, j, 0)),
                  pl.BlockSpec((1, 6, d), lambda i, j: (i, 0, 0)),
                  pl.BlockSpec((1, d), lambda i, j: (0, 0)),
                  pl.BlockSpec((e, d), lambda i, j: (0, 0))],
        out_specs=[pl.BlockSpec((1, tm, d), lambda i, j: (i, j, 0)),
                   pl.BlockSpec((1, e, tm), lambda i, j: (i, 0, j))],
        out_shape=[jax.ShapeDtypeStruct((b, n, d), BF16),
                   jax.ShapeDtypeStruct((b, e, n), F32)],
        name="moe_route",
        compiler_params=_cparams("arbitrary", "arbitrary"),
    )(x, mod, g2, rt)


def _topk_kernel(aff_ref, selpos_ref, gsel_ref, st_ref, stv_ref, most_ref, *, cap):
    aff = aff_ref[0]
    e, n = aff.shape
    nt = n // LANES
    capf = float(cap)

    def bit_step(i, tau):
        cand = tau | jnp.left_shift(jnp.int32(1), 30 - i)
        cnt = jnp.sum(jnp.where(aff >= pltpu.bitcast(cand, F32), 1.0, 0.0), axis=1, keepdims=True)
        return jnp.where(cnt >= capf, cand, tau)

    tau = lax.fori_loop(0, 31, bit_step, jnp.zeros((e, 1), I32))
    gt = aff >= pltpu.bitcast(tau + 1, F32)
    eq = (aff >= pltpu.bitcast(tau, F32)) & jnp.logical_not(gt)
    need = capf - jnp.sum(jnp.where(gt, 1.0, 0.0), axis=1, keepdims=True)

    r = lax.broadcasted_iota(I32, (LANES, LANES), 0)
    c = lax.broadcasted_iota(I32, (LANES, LANES), 1)
    upper = jnp.where(r < c, 1.0, 0.0).astype(BF16)

    def excl_prefix(mask_f):
        off = jnp.zeros((e, 1), F32)
        outs, starts = [], []
        for tb in range(nt):
            blk = mask_f[:, tb * LANES:(tb + 1) * LANES]
            outs.append(_dot(blk.astype(BF16), upper) + off)
            starts.append(off)
            off = off + jnp.sum(blk, axis=1, keepdims=True)
        return jnp.concatenate(outs, axis=1), starts

    eq_rank, _ = excl_prefix(jnp.where(eq, 1.0, 0.0))
    sel = gt | (eq & (eq_rank < need))
    pos, starts = excl_prefix(jnp.where(sel, 1.0, 0.0))
    selpos = jnp.where(sel, pos, -1.0)
    selpos_ref[0] = selpos.astype(I32)

    gsel_ref[0] = jnp.where(sel, aff, 0.0)

    lane = lax.broadcasted_iota(I32, (e, nt), 1)
    st = jnp.zeros((e, nt), F32)
    most = jnp.zeros((e, 1), F32)
    for tb in range(nt):
        st = jnp.where(lane == tb, starts[tb], st)
        stv_ref[0, tb] = starts[tb].astype(I32)
        nxt = starts[tb + 1] if tb + 1 < nt else capf
        most = jnp.maximum(most, nxt - starts[tb])
    st_ref[0] = st.astype(I32)
    most_ref[0] = jnp.max(most, axis=0, keepdims=True).astype(I32)


def _topk(aff_t, cap):
    b, e, n = aff_t.shape
    nt = n // LANES
    return pl.pallas_call(
        functools.partial(_topk_kernel, cap=cap),
        grid=(b,),
        in_specs=[pl.BlockSpec((1, e, n), lambda i: (i, 0, 0))],
        out_specs=[pl.BlockSpec((1, e, n), lambda i: (i, 0, 0)),
                   pl.BlockSpec((1, e, n), lambda i: (i, 0, 0)),
                   pl.BlockSpec((1, e, nt), lambda i: (i, 0, 0)),
                   pl.BlockSpec((1, nt, e, 1), lambda i: (i, 0, 0, 0)),
                   pl.BlockSpec((1, 1, 1), lambda i: (i, 0, 0))],
        out_shape=[jax.ShapeDtypeStruct((b, e, n), I32),
                   jax.ShapeDtypeStruct((b, e, n), F32),
                   jax.ShapeDtypeStruct((b, e, nt), I32),
                   jax.ShapeDtypeStruct((b, nt, e, 1), I32),
                   jax.ShapeDtypeStruct((b, 1, 1), I32)],
        name="moe_topk",
        compiler_params=_cparams("arbitrary"),
    )(aff_t)


def _gather_kernel(st_ref, most_ref, selpos_ref, gsel_ref, stv_ref, h_ref, xs_ref, gs_ref, *, cap):
    bi = pl.program_id(0)
    ne = xs_ref.shape[0]
    nt = h_ref.shape[1] // LANES
    first_cols = pl.program_id(1) == 0
    xs_ref[...] = jnp.zeros_like(xs_ref)

    @pl.when(first_cols)
    def _():
        gs_ref[...] = jnp.zeros_like(gs_ref)

    keep_gate = jnp.where(first_cols, 1.0, 0.0)

    def run(win, unroll):
        riota = lax.broadcasted_iota(I32, (ne, win, LANES), 1)

        def body(tb, _):
            rows = pl.ds(pl.multiple_of(tb * LANES, LANES), LANES)
            slot = selpos_ref[0, :, pl.ds(tb, 1), :]
            p0v = jnp.minimum(jnp.bitwise_and(stv_ref[0, tb], -BF16_ROWS), cap - win)
            onehot = jnp.where(slot - p0v[:, :, None] == riota, 1.0, 0.0)
            res = _dot(onehot.reshape(ne * win, LANES).astype(BF16), h_ref[0, rows, :])
            gate = jnp.sum(onehot * gsel_ref[0, :, pl.ds(tb, 1), :], axis=-1, keepdims=True) * keep_gate
            for ei in range(ne):
                start = st_ref[(bi * ne + ei) * nt + tb]
                p0 = pl.multiple_of(jnp.minimum(jnp.bitwise_and(start, -BF16_ROWS), cap - win), BF16_ROWS)
                xs_ref[ei, 0, pl.ds(p0, win), :] += res[ei * win:(ei + 1) * win].astype(xs_ref.dtype)
                gs_ref[ei, 0, pl.ds(p0, win), :] += gate[ei]
            return 0

        lax.fori_loop(0, nt, body, 0, unroll=unroll)

    fast = min(GATHER_FAST_WIN, cap)
    full = min(GATHER_WIN, cap)
    most = most_ref[bi]

    @pl.when(most <= fast - BF16_ROWS)
    def _():
        run(fast, min(2, nt))

    @pl.when(most > fast - BF16_ROWS)
    def _():
        run(full, 1)


def _gather(st, most, selpos, gsel, stv, h2, cap):
    b, n, d = h2.shape
    e = selpos.shape[1]
    nt = n // LANES
    dh = d // 2
    grid_spec = pltpu.PrefetchScalarGridSpec(
        num_scalar_prefetch=2,
        grid=(b, d // dh),
        in_specs=[pl.BlockSpec((1, e, nt, LANES), lambda i, k, p, q: (i, 0, 0, 0)),
                  pl.BlockSpec((1, e, nt, LANES), lambda i, k, p, q: (i, 0, 0, 0)),
                  pl.BlockSpec((1, nt, e, 1), lambda i, k, p, q: (i, 0, 0, 0)),
                  pl.BlockSpec((1, n, dh), lambda i, k, p, q: (i, 0, k))],
        out_specs=[pl.BlockSpec((e, 1, cap, dh), lambda i, k, p, q: (0, i, 0, k)),
                   pl.BlockSpec((e, 1, cap, 1), lambda i, k, p, q: (0, i, 0, 0))],
    )
    return pl.pallas_call(
        functools.partial(_gather_kernel, cap=cap),
        grid_spec=grid_spec,
        out_shape=[jax.ShapeDtypeStruct((e, b, cap, d), BF16),
                   jax.ShapeDtypeStruct((e, b, cap, 1), F32)],
        name="moe_gather",
        compiler_params=_cparams("arbitrary", "arbitrary"),
    )(st.reshape(-1), most.reshape(-1), selpos.reshape(b, e, nt, LANES), gsel.reshape(b, e, nt, LANES), stv, h2)


def _ffn_kernel(*refs, ngroups):
    xs = refs[:ngroups]
    gs = refs[ngroups:2 * ngroups]
    wg_ref, wu_ref, wd_ref = refs[2 * ngroups:2 * ngroups + 3]
    ys = refs[2 * ngroups + 3:3 * ngroups + 3]
    accs = refs[3 * ngroups + 3:]
    f = pl.program_id(1)
    wg = wg_ref[0, 0].astype(BF16)
    wu = wu_ref[0, 0].astype(BF16)
    wd = wd_ref[0, 0].astype(BF16)
    for x_ref, g_ref, y_ref, acc_ref in zip(xs, gs, ys, accs):
        rows = x_ref.shape[1]
        tr = min(512, rows)

        @pl.when(f == 0)
        def _(acc_ref=acc_ref):
            acc_ref[...] = jnp.zeros_like(acc_ref)

        for r0 in range(0, rows, tr):
            x = x_ref[0, r0:r0 + tr, :]
            g = _dot(x, wg)
            u = _dot(x, wu)
            acc_ref[r0:r0 + tr, :] += _dot((g * _sigmoid(g) * u).astype(BF16), wd)

        @pl.when(f == pl.num_programs(1) - 1)
        def _(y_ref=y_ref, acc_ref=acc_ref, g_ref=g_ref):
            y_ref[0] = (acc_ref[...] * g_ref[0]).astype(y_ref.dtype)


def _ffn(xs_list, gs_list, wg, wu, wd, layer):
    _, e, d, ff = wg.shape
    tf = min(512, ff)
    ng = len(xs_list)
    in_specs = [pl.BlockSpec((1, x.shape[1], d), lambda i, j: (i, 0, 0)) for x in xs_list]
    in_specs += [pl.BlockSpec((1, x.shape[1], 1), lambda i, j: (i, 0, 0)) for x in xs_list]
    in_specs += [pl.BlockSpec((1, 1, d, tf), lambda i, j: (layer, i, 0, j)),
                 pl.BlockSpec((1, 1, d, tf), lambda i, j: (layer, i, 0, j)),
                 pl.BlockSpec((1, 1, tf, d), lambda i, j: (layer, i, j, 0))]
    return pl.pallas_call(
        functools.partial(_ffn_kernel, ngroups=ng),
        grid=(e, ff // tf),
        in_specs=in_specs,
        out_specs=[pl.BlockSpec((1, x.shape[1], d), lambda i, j: (i, 0, 0)) for x in xs_list],
        out_shape=[jax.ShapeDtypeStruct(x.shape, BF16) for x in xs_list],
        scratch_shapes=[pltpu.VMEM((x.shape[1], d), F32) for x in xs_list],
        name="moe_ffn",
        compiler_params=_cparams("arbitrary", "arbitrary"),
    )(*xs_list, *gs_list, wg, wu, wd)


def _combine_kernel(st_ref, most_ref, selpos_ref, stv_ref, y_ref, x_ref, mod_ref, o_ref, ystk_ref, *, cap, nt_total):
    bi = pl.program_id(0)
    tg = pl.program_id(2)
    ne = y_ref.shape[0]
    tm = x_ref.shape[1]
    mgate = mod_ref[0][5:6]
    fwin = min(SCATTER_FAST_WIN, cap)
    swin = min(SCATTER_WIN, cap)

    def window_start(ei, tb, win):
        start = st_ref[(bi * ne + ei) * nt_total + tb]
        return pl.multiple_of(jnp.minimum(jnp.bitwise_and(start, -BF16_ROWS), cap - win), BF16_ROWS)

    def fast_body(t, _):
        tb = tg * (tm // LANES) + t
        rows = pl.ds(pl.multiple_of(t * LANES, LANES), LANES)
        for ei in range(ne):
            p0 = window_start(ei, tb, fwin)
            ystk_ref[ei * fwin:(ei + 1) * fwin, :] = y_ref[ei, 0, pl.ds(p0, fwin), :]
        slot = selpos_ref[0, :, pl.ds(tb, 1), :]
        p0v = jnp.minimum(jnp.bitwise_and(stv_ref[0, tb], -BF16_ROWS), cap - fwin)
        riota = lax.broadcasted_iota(I32, (ne, fwin, LANES), 1)
        onehot_t = jnp.where(slot - p0v[:, :, None] == riota, 1.0, 0.0)
        acc = _tn(onehot_t.reshape(ne * fwin, LANES).astype(BF16), ystk_ref[...])
        o_ref[0, rows, :] = x_ref[0, rows, :] + mgate * acc
        return 0

    def full_body(t, _):
        tb = tg * (tm // LANES) + t
        rows = pl.ds(pl.multiple_of(t * LANES, LANES), LANES)
        riota = lax.broadcasted_iota(I32, (swin, LANES), 0)
        acc = jnp.zeros((LANES, x_ref.shape[2]), F32)
        for ei in range(ne):
            p0 = window_start(ei, tb, swin)
            slot = selpos_ref[0, ei, pl.ds(tb, 1), :]
            onehot_t = jnp.where(slot - p0 == riota, 1.0, 0.0).astype(BF16)
            acc = acc + _tn(onehot_t, y_ref[ei, 0, pl.ds(p0, swin), :])
        o_ref[0, rows, :] = x_ref[0, rows, :] + mgate * acc
        return 0

    most = most_ref[bi]

    @pl.when(most <= fwin - BF16_ROWS)
    def _():
        lax.fori_loop(0, tm // LANES, fast_body, 0)

    @pl.when(most > fwin - BF16_ROWS)
    def _():
        lax.fori_loop(0, tm // LANES, full_body, 0)


def _combine(st, most, selpos, stv, y, x, mod, cap):
    b, n, d = x.shape
    e = y.shape[0]
    tm = min(512, n)
    dh = d // 2
    nt = n // LANES
    grid_spec = pltpu.PrefetchScalarGridSpec(
        num_scalar_prefetch=2,
        grid=(b, d // dh, n // tm),
        in_specs=[pl.BlockSpec((1, e, nt, LANES), lambda i, k, j, p, q: (i, 0, 0, 0)),
                  pl.BlockSpec((1, nt, e, 1), lambda i, k, j, p, q: (i, 0, 0, 0)),
                  pl.BlockSpec((e, 1, cap, dh), lambda i, k, j, p, q: (0, i, 0, k)),
                  pl.BlockSpec((1, tm, dh), lambda i, k, j, p, q: (i, j, k)),
                  pl.BlockSpec((1, 6, dh), lambda i, k, j, p, q: (i, 0, k))],
        out_specs=pl.BlockSpec((1, tm, dh), lambda i, k, j, p, q: (i, j, k)),
        scratch_shapes=[pltpu.VMEM((e * min(SCATTER_FAST_WIN, cap), dh), BF16)],
    )
    return pl.pallas_call(
        functools.partial(_combine_kernel, cap=cap, nt_total=nt),
        grid_spec=grid_spec,
        out_shape=jax.ShapeDtypeStruct((b, n, d), F32),
        name="moe_combine",
        compiler_params=_cparams("arbitrary", "arbitrary", "arbitrary"),
    )(st.reshape(-1), most.reshape(-1), selpos.reshape(b, e, nt, LANES), stv, y, x, mod)


def _final_kernel(x_ref, g_ref, o_ref):
    o_ref[0] = _rms(x_ref[0], g_ref[...])


def _final_norm(x, g):
    b, n, d = x.shape
    tm = min(1024, n)
    return pl.pallas_call(
        _final_kernel,
        grid=(b, n // tm),
        in_specs=[pl.BlockSpec((1, tm, d), lambda i, j: (i, j, 0)),
                  pl.BlockSpec((1, d), lambda i, j: (0, 0))],
        out_specs=pl.BlockSpec((1, tm, d), lambda i, j: (i, j, 0)),
        out_shape=jax.ShapeDtypeStruct((b, n, d), F32),
        name="final_norm",
        compiler_params=_cparams("arbitrary", "arbitrary"),
    )(x, g)


def _rot_cols(w):
    q = MLA_ROPE // 4
    return jnp.concatenate([-w[:, q:2 * q], w[:, :q], -w[:, 3 * q:], w[:, 2 * q:3 * q]], axis=1)


def _prep_even(w_in, wq_b, wkv_b):
    kpe = w_in[:, -MLA_ROPE:]
    win = jnp.concatenate([w_in, _rot_cols(kpe)], axis=1).astype(BF16)
    qh = wq_b.reshape(MLA_Q_RANK, MLA_HEADS, MLA_NOPE + MLA_ROPE)
    nope = [qh[:, h, :MLA_NOPE] for h in range(MLA_HEADS)]
    pe = [jnp.concatenate([qh[:, h, MLA_NOPE:], _rot_cols(qh[:, h, MLA_NOPE:])], axis=1) for h in range(MLA_HEADS)]
    wq = jnp.concatenate(nope + pe, axis=1).astype(BF16)
    kvh = wkv_b.reshape(MLA_KV_RANK, MLA_HEADS, MLA_NOPE + MLA_V)
    wkv = jnp.concatenate([kvh[:, h, :MLA_NOPE] for h in range(MLA_HEADS)]
                          + [kvh[:, h, MLA_NOPE:] for h in range(MLA_HEADS)], axis=1).astype(BF16)
    return win, wq, wkv


def _rope_tables(n):
    rows = n // GRID_W
    row = jnp.repeat(jnp.arange(rows, dtype=F32), GRID_W)
    col = jnp.tile(jnp.arange(GRID_W, dtype=F32), rows)
    half = MLA_ROPE // 2
    inv = 1.0 / (ROPE_THETA ** (jnp.arange(0, half, 2, dtype=F32) / half))
    ar = row[:, None] * inv[None, :]
    ac = col[:, None] * inv[None, :]
    ang = jnp.concatenate([ar, ar, ac, ac], axis=-1)
    z = jnp.zeros((n, LANES - MLA_ROPE), F32)
    return jnp.concatenate([jnp.cos(ang), z], axis=1), jnp.concatenate([jnp.sin(ang), z], axis=1)


def _moe(x, mod, g2, rt, wg, wu, wd, layer, xc=None, modc=None):
    streams = [(x, mod)] + ([(xc, modc)] if xc is not None else [])
    routed = []
    for xx, mm in streams:
        b, n, d = xx.shape
        e = rt.shape[0]
        cap = EC_FACTOR * n // e
        h2, aff_t = _route(xx, mm, g2, rt)
        selpos, gsel, st, stv, most = _topk(aff_t, cap)
        xs, gs = _gather(st, most, selpos, gsel, stv, h2, cap)
        routed.append((st, most, selpos, stv, cap, xs, gs))
    flat = lambda a: a.reshape(a.shape[0], -1, a.shape[3])
    ys = _ffn([flat(r[5]) for r in routed], [flat(r[6]) for r in routed], wg, wu, wd, layer)
    outs = []
    for (xx, mm), (st, most, selpos, stv, cap, xs, gs), y in zip(streams, routed, ys):
        outs.append(_combine(st, most, selpos, stv, y.reshape(xs.shape), xx, mm, cap))
    return outs if xc is not None else outs[0]


def kernel(x, c, ctx, c_ctx, ada_w, ada_b, norm1_g, norm2_g, w_in, hg_lb, hg_norm_g, mla_qn_g, mla_wq_b,
           mla_kvn_g, mla_wkv_b, w_out, pool_w, pool_scale, router_w, exp_wg, exp_wu, exp_wd, final_g):
    b, n, d = x.shape
    nctx = ctx.shape[1]
    depth = ada_w.shape[0]
    n_even = w_in.shape[0]
    assert b + 1 <= 8 and n % LANES == 0 and nctx % LANES == 0 and n % GRID_W == 0

    cc = jnp.zeros((8, d), F32).at[:b].set(c).at[b].set(c_ctx)
    mods = _ada(cc, ada_w, ada_b)
    mod_lat = [mods[l, :b].reshape(b, 6, d) for l in range(depth)]
    mod_ctx = [jnp.broadcast_to(mods[l, b].reshape(1, 6, d), (b, 6, d)) for l in range(depth)]

    lb_all = jnp.cumsum(jax.nn.softmax(hg_lb.astype(F32), axis=0), axis=0)
    lb_all = lb_all - lb_all[:1]
    cos1, sin1 = _rope_tables(n)
    cos_c = jnp.concatenate([jnp.ones((nctx, MLA_ROPE), F32), jnp.zeros((nctx, LANES - MLA_ROPE), F32)], axis=1)
    sin_c = jnp.zeros((nctx, LANES), F32)

    last_reader = 2 * (n_even - 1)
    row = lambda v: v.reshape(1, -1)
    x_lat, x_ctx = x, ctx
    for l in range(depth):
        j = l // 2
        ctx_in = l <= last_reader
        ctx_out = l < last_reader
        if l % 2 == 0:
            win, wq, wkv = _prep_even(w_in[j], mla_wq_b[j], mla_wkv_b[j])
            lb = lb_all[j]
            lbp = jnp.stack([lb[0], 1.0 - lb[0], lb[1], 1.0 - lb[1]])
            common = (row(norm1_g[l]), win, row(mla_qn_g[j]), wq, row(mla_kvn_g[j]), wkv)
            hg_l, q_l, k_l, v_l = _in_proj(x_lat, mod_lat[l], *common, cos1, sin1)
            hg_c = None
            ks, vs = [k_l], [v_l]
            if ctx_in:
                hg_c, q_c, k_c, v_c = _in_proj(x_ctx, mod_ctx[l], *common, cos_c, sin_c)
                ks, vs = [k_c, k_l], [v_c, v_l]
            hgo_l, hgo_c = _hgrn(hg_l, hg_c, lbp, row(hg_norm_g[j]), ctx_out)
            mla_l = _attention(q_l, ks, vs)
            wo = w_out[j].astype(BF16)
            x_lat = _out_proj(hgo_l, mla_l, x_lat, mod_lat[l], wo)
            if ctx_out:
                mla_c = _attention(q_c, [k_c], [v_c])
                x_ctx = _out_proj(hgo_c, mla_c, x_ctx, mod_ctx[l], wo)
        else:
            pw = pool_w[j].astype(BF16)
            x_lat = _pool(x_lat, mod_lat[l], row(norm1_g[l]), pw, row(pool_scale[j]))
            if ctx_out:
                x_ctx = _pool(x_ctx, mod_ctx[l], row(norm1_g[l]), pw, row(pool_scale[j]))
        rt = router_w[l].T
        if ctx_out:
            x_lat, x_ctx = _moe(x_lat, mod_lat[l], row(norm2_g[l]), rt, exp_wg, exp_wu, exp_wd, l,
                                x_ctx, mod_ctx[l])
        else:
            x_lat = _moe(x_lat, mod_lat[l], row(norm2_g[l]), rt, exp_wg, exp_wu, exp_wd, l)
    return _final_norm(x_lat, row(final_g))
```

```python
import functools

import numpy as np
import jax
import jax.numpy as jnp
from jax import lax
from jax.experimental import pallas as pl
from jax.experimental.pallas import tpu as pltpu

F32 = jnp.float32
BF16 = jnp.bfloat16
I32 = jnp.int32

EPS = 1e-6
GRID_W = 64
ROPE_THETA = 10000.0
HG_HEADS = 4
HG_D = 128
MLA_HEADS = 4
MLA_Q_RANK = 256
MLA_KV_RANK = 128
MLA_NOPE = 128
MLA_ROPE = 64
MLA_V = 128
MLA_SCALE = (MLA_NOPE + MLA_ROPE) ** -0.5
POOL_WINDOWS = (2, 4, 8, 16)
EC_FACTOR = 2
HG_W = HG_HEADS * HG_D

LANES = 128
CHUNK = 64
HG_HEADS_PER_STEP = 1
HG_CHAINS = 8
POOL_HALO = 8
SUBLANES = 8
BF16_ROWS = 16
GATHER_WIN = LANES + BF16_ROWS
GATHER_FAST_WIN = 48
SCATTER_WIN = 256
SCATTER_FAST_WIN = 64
ATTN_Q_SUB = 512
ATTN_Q_TILE = 1024
LOG2E = 1.4426950408889634
VMEM_LIMIT_V7X = 56 * 1024 * 1024


def _cparams(*sem):
    return pltpu.CompilerParams(dimension_semantics=sem, vmem_limit_bytes=VMEM_LIMIT_V7X)


def _nt(a, b):
    return lax.dot_general(a, b, (((1,), (1,)), ((), ())), preferred_element_type=F32)


def _tn(a, b):
    return lax.dot_general(a, b, (((0,), (0,)), ((), ())), preferred_element_type=F32)


def _dot(a, b):
    return jnp.dot(a, b, preferred_element_type=F32)


def _sigmoid(x):
    return 1.0 / (1.0 + jnp.exp(-x))


def _rms(x, g):
    return x * lax.rsqrt(jnp.mean(x * x, axis=-1, keepdims=True) + EPS) * g


def _modulate(x, g, shift, scale):
    return _rms(x, g) * (1.0 + scale) + shift


def _split3(x):
    hi = x.astype(BF16)
    r1 = x - hi.astype(F32)
    mid = r1.astype(BF16)
    lo = (r1 - mid.astype(F32)).astype(BF16)
    return hi, mid, lo


def _ada_kernel(c_ref, w_ref, b_ref, o_ref):
    c = c_ref[...]
    s = (c * _sigmoid(c)).astype(BF16)
    o_ref[0] = _dot(s, w_ref[0].astype(BF16)) + b_ref[0]


def _ada(cc, ada_w, ada_b):
    depth, d, d6 = ada_w.shape
    tn = d6 // 4
    return pl.pallas_call(
        _ada_kernel,
        grid=(depth, d6 // tn),
        in_specs=[pl.BlockSpec((8, d), lambda l, j: (0, 0)),
                  pl.BlockSpec((1, d, tn), lambda l, j: (l, 0, j)),
                  pl.BlockSpec((1, 1, tn), lambda l, j: (l, 0, j))],
        out_specs=pl.BlockSpec((1, 8, tn), lambda l, j: (l, 0, j)),
        out_shape=jax.ShapeDtypeStruct((depth, 8, d6), F32),
        name="ada_mod",
        compiler_params=_cparams("arbitrary", "arbitrary"),
    )(cc, ada_w, ada_b.reshape(depth, 1, d6))


def _in_kernel(x_ref, mod_ref, g1_ref, win_ref, qng_ref, wq_ref, kvg_ref, wkv_ref, cs1_ref, cs2_ref,
               hg_ref, q_ref, k_ref, vt_ref):
    m = mod_ref[0]
    h = _modulate(x_ref[0], g1_ref[...], m[0:1], m[1:2]).astype(BF16)
    proj = _dot(h, win_ref[...])
    o = 5 * HG_W
    hg_ref[0] = proj[:, :o]
    qa = proj[:, o:o + MLA_Q_RANK]
    kva = proj[:, o + MLA_Q_RANK:o + MLA_Q_RANK + MLA_KV_RANK]
    kp = proj[:, o + MLA_Q_RANK + MLA_KV_RANK:]
    qup = _dot(_rms(qa, qng_ref[...]).astype(BF16), wq_ref[...])
    kvup = _dot(_rms(kva, kvg_ref[...]).astype(BF16), wkv_ref[...])
    cs1 = cs1_ref[...]
    cs2 = cs2_ref[...]
    half = LANES // 2
    kpe = (kp * cs1 + pltpu.roll(kp, half, axis=1) * cs2).astype(BF16)
    nn = MLA_HEADS * MLA_NOPE
    qs = MLA_SCALE * LOG2E
    for hd in range(MLA_HEADS):
        p = qup[:, nn + LANES * hd:nn + LANES * (hd + 1)]
        qpe = p * cs1 + pltpu.roll(p, half, axis=1) * cs2
        q_ref[0, :, 2 * LANES * hd:2 * LANES * hd + LANES] = (qup[:, LANES * hd:LANES * (hd + 1)] * qs).astype(BF16)
        q_ref[0, :, 2 * LANES * hd + LANES:2 * LANES * (hd + 1)] = (qpe * qs).astype(BF16)
        k_ref[0, :, 2 * LANES * hd:2 * LANES * hd + LANES] = kvup[:, LANES * hd:LANES * (hd + 1)].astype(BF16)
        k_ref[0, :, 2 * LANES * hd + LANES:2 * LANES * (hd + 1)] = kpe
    vt_ref[0] = kvup[:, nn:].T.astype(BF16)


def _in_proj(x, mod, g1, win, qng, wq, kvg, wkv, cs1, cs2):
    b, n, d = x.shape
    tm = min(256, n)
    wcols = win.shape[1]
    full = lambda shape: pl.BlockSpec(shape, lambda i, j: (0,) * len(shape))
    return pl.pallas_call(
        _in_kernel,
        grid=(b, n // tm),
        in_specs=[pl.BlockSpec((1, tm, d), lambda i, j: (i, j, 0)),
                  pl.BlockSpec((1, 6, d), lambda i, j: (i, 0, 0)),
                  full((1, d)), full((d, wcols)), full((1, MLA_Q_RANK)), full(wq.shape),
                  full((1, MLA_KV_RANK)), full(wkv.shape),
                  pl.BlockSpec((tm, LANES), lambda i, j: (j, 0)),
                  pl.BlockSpec((tm, LANES), lambda i, j: (j, 0))],
        out_specs=[pl.BlockSpec((1, tm, 5 * HG_W), lambda i, j: (i, j, 0)),
                   pl.BlockSpec((1, tm, 2 * LANES * MLA_HEADS), lambda i, j: (i, j, 0)),
                   pl.BlockSpec((1, tm, 2 * LANES * MLA_HEADS), lambda i, j: (i, j, 0)),
                   pl.BlockSpec((1, MLA_HEADS * MLA_V, tm), lambda i, j: (i, 0, j))],
        out_shape=[jax.ShapeDtypeStruct((b, n, 5 * HG_W), F32),
                   jax.ShapeDtypeStruct((b, n, 2 * LANES * MLA_HEADS), BF16),
                   jax.ShapeDtypeStruct((b, n, 2 * LANES * MLA_HEADS), BF16),
                   jax.ShapeDtypeStruct((b, MLA_HEADS * MLA_V, n), BF16)],
        name="in_proj",
        compiler_params=_cparams("arbitrary", "arbitrary"),
    )(x, mod, g1, win, qng, wq, kvg, wkv, cs1, cs2)


def _attn_kernel(*refs, nparts):
    q_ref = refs[0]
    k_refs = refs[1:1 + nparts]
    v_refs = refs[1 + nparts:1 + 2 * nparts]
    o_ref = refs[1 + 2 * nparts]
    tq = q_ref.shape[1]
    sub = min(ATTN_Q_SUB, tq)
    qk = lambda q0: [_nt(k[0], q_ref[0, q0:q0 + sub, :]) for k in k_refs]
    nxt = qk(0)
    for q0 in range(0, tq, sub):
        s = nxt
        if q0 + sub < tq:
            nxt = qk(q0 + sub)
        m = functools.reduce(jnp.maximum, [jnp.max(x, axis=0, keepdims=True) for x in s])
        p = [jnp.exp2(x - m) for x in s]
        den = functools.reduce(jnp.add, [jnp.sum(x, axis=0, keepdims=True) for x in p])
        acc = functools.reduce(jnp.add, [_dot(vt[0], x.astype(BF16)) for vt, x in zip(v_refs, p)])
        o_ref[0, q0:q0 + sub, :] = (acc / den).T.astype(o_ref.dtype)


def _attention(q, ks, vts):
    b, n, _ = q.shape
    tq = min(ATTN_Q_TILE, n)
    nparts = len(ks)
    in_specs = [pl.BlockSpec((1, tq, 2 * LANES), lambda i, h, j: (i, j, h))]
    in_specs += [pl.BlockSpec((1, k.shape[1], 2 * LANES), lambda i, h, j: (i, 0, h)) for k in ks]
    in_specs += [pl.BlockSpec((1, MLA_V, v.shape[2]), lambda i, h, j: (i, h, 0)) for v in vts]
    return pl.pallas_call(
        functools.partial(_attn_kernel, nparts=nparts),
        grid=(b, MLA_HEADS, n // tq),
        in_specs=in_specs,
        out_specs=pl.BlockSpec((1, tq, MLA_V), lambda i, h, j: (i, j, h)),
        out_shape=jax.ShapeDtypeStruct((b, n, MLA_HEADS * MLA_V), BF16),
        name="mla_attention",
        compiler_params=_cparams("arbitrary", "arbitrary", "arbitrary"),
    )(q, *ks, *vts)


_HG_HALVES = (1, 2, 4, 8, 16, 32)


def _hgrn_masks():
    t = np.arange(CHUNK)[:, None]
    s = np.arange(CHUNK)[None, :]
    masks_f = [(t == s)]
    for h in _HG_HALVES:
        masks_f.append((t // (2 * h) == s // (2 * h)) & (t % (2 * h) >= h) & (s % (2 * h) < h))
    masks_f = np.stack(masks_f).astype(np.float32)
    return masks_f, np.transpose(masks_f, (0, 2, 1))


def _hg_block_decays(f, fw):
    c = CHUNK
    pre, suf = [f], [jnp.ones_like(f)]
    h = 1
    while h < c:
        blk = 2 * h
        src_early = h - 1 if fw else h
        src_late = blk - 1 if fw else 0
        p = pre[-1]
        if blk <= SUBLANES:
            p3 = p.reshape(c // SUBLANES, SUBLANES, HG_D)
            r = lax.broadcasted_iota(I32, p3.shape, 1)
            tot_e = jnp.ones_like(p3)
            tot_l = jnp.ones_like(p3)
            for kb in range(SUBLANES // blk):
                inblk = jnp.right_shift(r, blk.bit_length() - 1) == kb
                tot_e = jnp.where(inblk, p3[:, kb * blk + src_early:kb * blk + src_early + 1, :], tot_e)
                tot_l = jnp.where(inblk, p3[:, kb * blk + src_late:kb * blk + src_late + 1, :], tot_l)
        else:
            p3 = p.reshape(c // blk, blk, HG_D)
            r = lax.broadcasted_iota(I32, p3.shape, 1)
            tot_e = jnp.broadcast_to(p3[:, src_early:src_early + 1, :], p3.shape)
            tot_l = jnp.broadcast_to(p3[:, src_late:src_late + 1, :], p3.shape)
        in_first = jnp.bitwise_and(r, blk - 1) < h
        late = jnp.logical_not(in_first) if fw else in_first
        s3 = suf[-1].reshape(p3.shape)
        pre.append(jnp.where(late, p3 * tot_e, p3).reshape(c, HG_D))
        suf.append(jnp.where(late, s3, s3 * tot_l).reshape(c, HG_D))
        h = blk
    return pre, suf


def _hg_chunk(q, z, v, lb, onemlb, masks, st, fw, want_out):
    t = jnp.exp(-jnp.abs(z))
    r = 1.0 / (1.0 + t)
    pos = z >= 0.0
    f = lb + onemlb * jnp.where(pos, r, t * r)
    k = onemlb * jnp.where(pos, t * r, r)
    pre, suf = _hg_block_decays(f, fw)
    vb = v.astype(BF16)
    whole = pre[-1]
    kdl = (k * suf[-1]).astype(BF16)
    last = CHUNK - 1 if fw else 0
    dec = whole[last:last + 1]
    out = None
    if want_out:
        kb = k.astype(BF16)
        att = masks[0] * _nt(q.astype(BF16), kb)
        for i in range(len(_HG_HALVES)):
            a = (q * pre[i]).astype(BF16)
            bm = kb if i == 0 else (k * suf[i]).astype(BF16)
            att = att + masks[1 + i] * _nt(a, bm)
        out = _dot(att.astype(BF16), vb) + _nt((q * whole).astype(BF16), st.astype(BF16))
    st_new = st * dec + _tn(vb, kdl)
    return out, st_new


def _hgrn_kernel(*refs, has_ctx, ctx_out):
    it = iter(refs)
    q_ref, zf_ref, zb_ref, v_ref, g_ref = (next(it) for _ in range(5))
    if has_ctx:
        qc_ref, zfc_ref, zbc_ref, vc_ref, gc_ref = (next(it) for _ in range(5))
    lb_ref, gn_ref, mkf_ref, mkb_ref = (next(it) for _ in range(4))
    o_ref = next(it)
    oc_ref = next(it) if ctx_out else None
    of_s, ob_s = next(it), next(it)
    if ctx_out:
        ocf_s, ocb_s = next(it), next(it)

    lbp = lb_ref[...]
    masks = (mkf_ref[...], mkb_ref[...])
    zrefs = (zf_ref, zb_ref)

    def seq_pass(q_r, z_rs, v_r, o_ss, n, states, want_out):
        nc = n // CHUNK

        def body(ci, carry):
            new = []
            for hd in range(HG_HEADS_PER_STEP):
                cols = slice(hd * HG_D, (hd + 1) * HG_D)
                for d in range(2):
                    cidx = ci if d == 0 else nc - 1 - ci
                    off = pl.multiple_of(cidx * CHUNK, CHUNK)
                    rows = pl.ds(off, CHUNK)
                    qv = q_r[0, rows, cols] if want_out else None
                    out, st = _hg_chunk(qv, z_rs[d][0, rows, cols], v_r[0, rows, cols],
                                        lbp[2 * d:2 * d + 1, cols], lbp[2 * d + 1:2 * d + 2, cols],
                                        masks[d], carry[2 * hd + d], d == 0, want_out)
                    if want_out:
                        o_ss[d][rows, cols] = out
                    new.append(st)
            return tuple(new)

        return lax.fori_loop(0, nc, body, states, unroll=HG_CHAINS // (2 * HG_HEADS_PER_STEP))

    def readout(o_ss, g_r, out_r, n):
        tr = min(512, n)

        def body(i, _):
            rows = pl.ds(pl.multiple_of(i * tr, tr), tr)
            for hd in range(HG_HEADS_PER_STEP):
                cols = slice(hd * HG_D, (hd + 1) * HG_D)
                o = o_ss[0][rows, cols] + o_ss[1][rows, cols]
                g = g_r[0, rows, cols]
                out_r[0, rows, cols] = (_rms(o, gn_ref[...]) * (g * _sigmoid(g))).astype(out_r.dtype)
            return 0

        lax.fori_loop(0, n // tr, body, 0)

    zero = jnp.zeros((HG_D, HG_D), F32)
    states = (zero,) * (2 * HG_HEADS_PER_STEP)
    if has_ctx:
        nctx = vc_ref.shape[1]
        states = seq_pass(qc_ref, (zfc_ref, zbc_ref), vc_ref, (ocf_s, ocb_s) if ctx_out else None,
                          nctx, states, ctx_out)
        if ctx_out:
            readout((ocf_s, ocb_s), gc_ref, oc_ref, nctx)
    n = v_ref.shape[1]
    seq_pass(q_ref, zrefs, v_ref, (of_s, ob_s), n, states, True)
    readout((of_s, ob_s), g_ref, o_ref, n)


def _hgrn(hg, hg_ctx, lbp, gn, ctx_out):
    b, n, _ = hg.shape
    has_ctx = hg_ctx is not None
    consts = [jnp.asarray(a, F32) for a in _hgrn_masks()]

    hw = HG_HEADS_PER_STEP * HG_D
    groups = HG_HEADS // HG_HEADS_PER_STEP

    def part(nn, k):
        return pl.BlockSpec((1, nn, hw), lambda i, h, k=k: (i, 0, k * groups + h))

    in_specs = [part(n, k) for k in range(5)]
    args = [hg] * 5
    if has_ctx:
        nctx = hg_ctx.shape[1]
        in_specs += [part(nctx, k) for k in range(5)]
        args += [hg_ctx] * 5
    in_specs += [pl.BlockSpec((4, hw), lambda i, h: (0, h)),
                 pl.BlockSpec((1, HG_D), lambda i, h: (0, 0))]
    in_specs += [pl.BlockSpec(a.shape, lambda i, h, nd=a.ndim: (0,) * nd) for a in consts]
    args += [lbp, gn] + consts
    out_specs = [pl.BlockSpec((1, n, hw), lambda i, h: (i, 0, h))]
    out_shape = [jax.ShapeDtypeStruct((b, n, HG_W), BF16)]
    scratch = [pltpu.VMEM((n, hw), F32), pltpu.VMEM((n, hw), F32)]
    if ctx_out:
        out_specs.append(pl.BlockSpec((1, nctx, hw), lambda i, h: (i, 0, h)))
        out_shape.append(jax.ShapeDtypeStruct((b, nctx, HG_W), BF16))
        scratch += [pltpu.VMEM((nctx, hw), F32), pltpu.VMEM((nctx, hw), F32)]
    res = pl.pallas_call(
        functools.partial(_hgrn_kernel, has_ctx=has_ctx, ctx_out=ctx_out),
        grid=(b, groups),
        in_specs=in_specs, out_specs=out_specs, out_shape=out_shape, scratch_shapes=scratch,
        name="hgrn_scan",
        compiler_params=_cparams("arbitrary", "arbitrary"),
    )(*args)
    return (res[0], res[1]) if ctx_out else (res[0], None)


def _route_tile(x, m, g2, rt):
    h = _modulate(x, g2, m[3:4], m[4:5])
    hb = h.astype(BF16)
    hlo = (h - hb.astype(F32)).astype(BF16)
    rhi = rt.astype(BF16)
    rlo = (rt - rhi.astype(F32)).astype(BF16)
    logits = _nt(rhi, hb) + (_nt(rhi, hlo) + _nt(rlo, hb))
    ex = jnp.exp(logits - jnp.max(logits, axis=0, keepdims=True))
    return hb, ex / jnp.sum(ex, axis=0, keepdims=True)


def _mixer_out_specs(b, n, d, e, tm):
    specs = [pl.BlockSpec((1, tm, d), lambda i, j: (i, j, 0)),
             pl.BlockSpec((1, tm, d), lambda i, j: (i, j, 0)),
             pl.BlockSpec((1, e, tm), lambda i, j: (i, 0, j))]
    shapes = [jax.ShapeDtypeStruct((b, n, d), F32),
              jax.ShapeDtypeStruct((b, n, d), BF16),
              jax.ShapeDtypeStruct((b, e, n), F32)]
    return specs, shapes


def _out_kernel(hg_ref, mla_ref, x_ref, mod_ref, w_ref, g2_ref, rt_ref, o_ref, h_ref, aff_ref):
    w = w_ref[...]
    m = mod_ref[0]
    y = _dot(hg_ref[0], w[:HG_W]) + _dot(mla_ref[0], w[HG_W:])
    xn = x_ref[0] + m[2:3] * y
    o_ref[0] = xn
    h_ref[0], aff_ref[0] = _route_tile(xn, m, g2_ref[...], rt_ref[...])


def _out_proj(hgo, mla, x, mod, w, g2, rt):
    b, n, d = x.shape
    e = rt.shape[0]
    tm = min(512, n)
    out_specs, out_shape = _mixer_out_specs(b, n, d, e, tm)
    return pl.pallas_call(
        _out_kernel,
        grid=(b, n // tm),
        in_specs=[pl.BlockSpec((1, tm, HG_W), lambda i, j: (i, j, 0)),
                  pl.BlockSpec((1, tm, MLA_HEADS * MLA_V), lambda i, j: (i, j, 0)),
                  pl.BlockSpec((1, tm, d), lambda i, j: (i, j, 0)),
                  pl.BlockSpec((1, 6, d), lambda i, j: (i, 0, 0)),
                  pl.BlockSpec(w.shape, lambda i, j: (0, 0)),
                  pl.BlockSpec((1, d), lambda i, j: (0, 0)),
                  pl.BlockSpec((e, d), lambda i, j: (0, 0))],
        out_specs=out_specs, out_shape=out_shape,
        name="out_proj",
        compiler_params=_cparams("arbitrary", "arbitrary"),
    )(hgo, mla, x, mod, w, g2, rt)


def _pool_kernel(x_ref, prev_ref, next_ref, mod_ref, g1_ref, pw_ref, ps_ref, g2_ref, rt_ref,
                 o_ref, h_ref, aff_ref, hs_ref, *, n):
    tm = x_ref.shape[1]
    d = x_ref.shape[2]
    gw = d // len(POOL_WINDOWS)
    m = mod_ref[0]
    base = pl.program_id(1) * tm
    xe = jnp.concatenate([prev_ref[0], x_ref[0], next_ref[0]], axis=0)
    he = _modulate(xe, g1_ref[...], m[0:1], m[1:2])
    gidx = base - POOL_HALO + lax.broadcasted_iota(I32, (tm + 2 * POOL_HALO, 1), 0)
    hs_ref[...] = jnp.where((gidx >= 0) & (gidx < n), he, 0.0)
    t = base + lax.broadcasted_iota(I32, (tm, 1), 0)
    ps = ps_ref[...]
    for gi, w in enumerate(POOL_WINDOWS):
        cols = slice(gi * gw, (gi + 1) * gw)
        acc = hs_ref[pl.ds(POOL_HALO - w // 2, tm), cols]
        for j in range(1 - w // 2, w // 2):
            acc = acc + hs_ref[pl.ds(POOL_HALO + j, tm), cols]
        lo = jnp.clip(t - w // 2, 0, n - 1)
        hi = jnp.clip(t + w // 2 - 1, 0, n - 1)
        pooled = acc / (hi - lo + 1).astype(F32) - hs_ref[pl.ds(POOL_HALO, tm), cols]
        y = _dot(pooled.astype(BF16), pw_ref[gi]) * ps[:, cols]
        o_ref[0, :, cols] = x_ref[0, :, cols] + m[2:3, cols] * y
    h_ref[0], aff_ref[0] = _route_tile(o_ref[0], m, g2_ref[...], rt_ref[...])


def _pool(x, mod, g1, pw, ps, g2, rt):
    b, n, d = x.shape
    e = rt.shape[0]
    tm = min(512, n)
    nb = tm // POOL_HALO
    last = n // POOL_HALO - 1
    out_specs, out_shape = _mixer_out_specs(b, n, d, e, tm)
    return pl.pallas_call(
        functools.partial(_pool_kernel, n=n),
        grid=(b, n // tm),
        in_specs=[pl.BlockSpec((1, tm, d), lambda i, j: (i, j, 0)),
                  pl.BlockSpec((1, POOL_HALO, d), lambda i, j: (i, jnp.maximum(j * nb - 1, 0), 0)),
                  pl.BlockSpec((1, POOL_HALO, d), lambda i, j: (i, jnp.minimum((j + 1) * nb, last), 0)),
                  pl.BlockSpec((1, 6, d), lambda i, j: (i, 0, 0)),
                  pl.BlockSpec((1, d), lambda i, j: (0, 0)),
                  pl.BlockSpec(pw.shape, lambda i, j: (0, 0, 0)),
                  pl.BlockSpec((1, d), lambda i, j: (0, 0)),
                  pl.BlockSpec((1, d), lambda i, j: (0, 0)),
                  pl.BlockSpec((e, d), lambda i, j: (0, 0))],
        out_specs=out_specs, out_shape=out_shape,
        scratch_shapes=[pltpu.VMEM((tm + 2 * POOL_HALO, d), F32)],
        name="pool_mixer",
        compiler_params=_cparams("arbitrary", "arbitrary"),
    )(x, x, x, mod, g1, pw, ps, g2, rt)


def _topk_kernel(aff_ref, selpos_ref, gsel_ref, st_ref, stv_ref, most_ref, *, cap):
    aff = aff_ref[0]
    e, n = aff.shape
    nt = n // LANES
    capf = float(cap)

    def bit_step(i, tau):
        cand = tau | jnp.left_shift(jnp.int32(1), 30 - i)
        cnt = jnp.sum(jnp.where(aff >= pltpu.bitcast(cand, F32), 1.0, 0.0), axis=1, keepdims=True)
        return jnp.where(cnt >= capf, cand, tau)

    tau = lax.fori_loop(0, 31, bit_step, jnp.zeros((e, 1), I32))
    gt = aff >= pltpu.bitcast(tau + 1, F32)
    eq = (aff >= pltpu.bitcast(tau, F32)) & jnp.logical_not(gt)
    need = capf - jnp.sum(jnp.where(gt, 1.0, 0.0), axis=1, keepdims=True)

    r = lax.broadcasted_iota(I32, (LANES, LANES), 0)
    c = lax.broadcasted_iota(I32, (LANES, LANES), 1)
    upper = jnp.where(r < c, 1.0, 0.0).astype(BF16)

    def excl_prefix(mask_f):
        off = jnp.zeros((e, 1), F32)
        outs, starts = [], []
        for tb in range(nt):
            blk = mask_f[:, tb * LANES:(tb + 1) * LANES]
            outs.append(_dot(blk.astype(BF16), upper) + off)
            starts.append(off)
            off = off + jnp.sum(blk, axis=1, keepdims=True)
        return jnp.concatenate(outs, axis=1), starts

    eq_rank, _ = excl_prefix(jnp.where(eq, 1.0, 0.0))
    sel = gt | (eq & (eq_rank < need))
    pos, starts = excl_prefix(jnp.where(sel, 1.0, 0.0))
    selpos = jnp.where(sel, pos, -1.0)
    selpos_ref[0] = selpos.astype(I32)

    gsel_ref[0] = jnp.where(sel, aff, 0.0)

    lane = lax.broadcasted_iota(I32, (e, nt), 1)
    st = jnp.zeros((e, nt), F32)
    most = jnp.zeros((e, 1), F32)
    for tb in range(nt):
        st = jnp.where(lane == tb, starts[tb], st)
        stv_ref[0, tb] = starts[tb].astype(I32)
        nxt = starts[tb + 1] if tb + 1 < nt else capf
        most = jnp.maximum(most, nxt - starts[tb])
    st_ref[0] = st.astype(I32)
    most_ref[0] = jnp.max(most, axis=0, keepdims=True).astype(I32)


def _topk(aff_t, cap):
    b, e, n = aff_t.shape
    nt = n // LANES
    return pl.pallas_call(
        functools.partial(_topk_kernel, cap=cap),
        grid=(b,),
        in_specs=[pl.BlockSpec((1, e, n), lambda i: (i, 0, 0))],
        out_specs=[pl.BlockSpec((1, e, n), lambda i: (i, 0, 0)),
                   pl.BlockSpec((1, e, n), lambda i: (i, 0, 0)),
                   pl.BlockSpec((1, e, nt), lambda i: (i, 0, 0)),
                   pl.BlockSpec((1, nt, e, 1), lambda i: (i, 0, 0, 0)),
                   pl.BlockSpec((1, 1, 1), lambda i: (i, 0, 0))],
        out_shape=[jax.ShapeDtypeStruct((b, e, n), I32),
                   jax.ShapeDtypeStruct((b, e, n), F32),
                   jax.ShapeDtypeStruct((b, e, nt), I32),
                   jax.ShapeDtypeStruct((b, nt, e, 1), I32),
                   jax.ShapeDtypeStruct((b, 1, 1), I32)],
        name="moe_topk",
        compiler_params=_cparams("arbitrary"),
    )(aff_t)


def _gather_kernel(st_ref, most_ref, selpos_ref, gsel_ref, stv_ref, h_ref, xs_ref, gs_ref, *, cap):
    bi = pl.program_id(0)
    ne = xs_ref.shape[0]
    nt = h_ref.shape[1] // LANES
    first_cols = pl.program_id(1) == 0
    xs_ref[...] = jnp.zeros_like(xs_ref)

    @pl.when(first_cols)
    def _():
        gs_ref[...] = jnp.zeros_like(gs_ref)

    keep_gate = jnp.where(first_cols, 1.0, 0.0)

    def run(win, unroll):
        riota = lax.broadcasted_iota(I32, (ne, win, LANES), 1)

        def body(tb, _):
            rows = pl.ds(pl.multiple_of(tb * LANES, LANES), LANES)
            slot = selpos_ref[0, :, pl.ds(tb, 1), :]
            p0v = jnp.minimum(jnp.bitwise_and(stv_ref[0, tb], -BF16_ROWS), cap - win)
            onehot = jnp.where(slot - p0v[:, :, None] == riota, 1.0, 0.0)
            res = _dot(onehot.reshape(ne * win, LANES).astype(BF16), h_ref[0, rows, :])
            gate = jnp.sum(onehot * gsel_ref[0, :, pl.ds(tb, 1), :], axis=-1, keepdims=True) * keep_gate
            for ei in range(ne):
                start = st_ref[(bi * ne + ei) * nt + tb]
                p0 = pl.multiple_of(jnp.minimum(jnp.bitwise_and(start, -BF16_ROWS), cap - win), BF16_ROWS)
                xs_ref[ei, 0, pl.ds(p0, win), :] += res[ei * win:(ei + 1) * win].astype(xs_ref.dtype)
                gs_ref[ei, 0, pl.ds(p0, win), :] += gate[ei]
            return 0

        lax.fori_loop(0, nt, body, 0, unroll=unroll)

    fast = min(GATHER_FAST_WIN, cap)
    full = min(GATHER_WIN, cap)
    most = most_ref[bi]

    @pl.when(most <= fast - BF16_ROWS)
    def _():
        run(fast, min(2, nt))

    @pl.when(most > fast - BF16_ROWS)
    def _():
        run(full, 1)


def _gather(st, most, selpos, gsel, stv, h2, cap):
    b, n, d = h2.shape
    e = selpos.shape[1]
    nt = n // LANES
    dh = d // 2
    grid_spec = pltpu.PrefetchScalarGridSpec(
        num_scalar_prefetch=2,
        grid=(b, d // dh),
        in_specs=[pl.BlockSpec((1, e, nt, LANES), lambda i, k, p, q: (i, 0, 0, 0)),
                  pl.BlockSpec((1, e, nt, LANES), lambda i, k, p, q: (i, 0, 0, 0)),
                  pl.BlockSpec((1, nt, e, 1), lambda i, k, p, q: (i, 0, 0, 0)),
                  pl.BlockSpec((1, n, dh), lambda i, k, p, q: (i, 0, k))],
        out_specs=[pl.BlockSpec((e, 1, cap, dh), lambda i, k, p, q: (0, i, 0, k)),
                   pl.BlockSpec((e, 1, cap, 1), lambda i, k, p, q: (0, i, 0, 0))],
    )
    return pl.pallas_call(
        functools.partial(_gather_kernel, cap=cap),
        grid_spec=grid_spec,
        out_shape=[jax.ShapeDtypeStruct((e, b, cap, d), BF16),
                   jax.ShapeDtypeStruct((e, b, cap, 1), F32)],
        name="moe_gather",
        compiler_params=_cparams("arbitrary", "arbitrary"),
    )(st.reshape(-1), most.reshape(-1), selpos.reshape(b, e, nt, LANES), gsel.reshape(b, e, nt, LANES), stv, h2)


def _ffn_kernel(*refs, ngroups):
    xs = refs[:ngroups]
    gs = refs[ngroups:2 * ngroups]
    wg_ref, wu_ref, wd_ref = refs[2 * ngroups:2 * ngroups + 3]
    ys = refs[2 * ngroups + 3:3 * ngroups + 3]
    accs = refs[3 * ngroups + 3:]
    f = pl.program_id(1)
    wg = wg_ref[0, 0].astype(BF16)
    wu = wu_ref[0, 0].astype(BF16)
    wd = wd_ref[0, 0].astype(BF16)
    for x_ref, g_ref, y_ref, acc_ref in zip(xs, gs, ys, accs):
        rows = x_ref.shape[1]
        tr = min(512, rows)

        @pl.when(f == 0)
        def _(acc_ref=acc_ref):
            acc_ref[...] = jnp.zeros_like(acc_ref)

        for r0 in range(0, rows, tr):
            x = x_ref[0, r0:r0 + tr, :]
            g = _dot(x, wg)
            u = _dot(x, wu)
            acc_ref[r0:r0 + tr, :] += _dot((g * _sigmoid(g) * u).astype(BF16), wd)

        @pl.when(f == pl.num_programs(1) - 1)
        def _(y_ref=y_ref, acc_ref=acc_ref, g_ref=g_ref):
            y_ref[0] = (acc_ref[...] * g_ref[0]).astype(y_ref.dtype)


def _ffn(xs_list, gs_list, wg, wu, wd, layer):
    _, e, d, ff = wg.shape
    tf = min(512, ff)
    ng = len(xs_list)
    in_specs = [pl.BlockSpec((1, x.shape[1], d), lambda i, j: (i, 0, 0)) for x in xs_list]
    in_specs += [pl.BlockSpec((1, x.shape[1], 1), lambda i, j: (i, 0, 0)) for x in xs_list]
    in_specs += [pl.BlockSpec((1, 1, d, tf), lambda i, j: (layer, i, 0, j)),
                 pl.BlockSpec((1, 1, d, tf), lambda i, j: (layer, i, 0, j)),
                 pl.BlockSpec((1, 1, tf, d), lambda i, j: (layer, i, j, 0))]
    return pl.pallas_call(
        functools.partial(_ffn_kernel, ngroups=ng),
        grid=(e, ff // tf),
        in_specs=in_specs,
        out_specs=[pl.BlockSpec((1, x.shape[1], d), lambda i, j: (i, 0, 0)) for x in xs_list],
        out_shape=[jax.ShapeDtypeStruct(x.shape, BF16) for x in xs_list],
        scratch_shapes=[pltpu.VMEM((x.shape[1], d), F32) for x in xs_list],
        name="moe_ffn",
        compiler_params=_cparams("arbitrary", "arbitrary"),
    )(*xs_list, *gs_list, wg, wu, wd)


def _combine_kernel(st_ref, most_ref, selpos_ref, stv_ref, y_ref, x_ref, mod_ref, fg_ref, o_ref, ystk_ref, *,
                    cap, nt_total, final):
    bi = pl.program_id(0)
    tg = pl.program_id(2)
    ne = y_ref.shape[0]
    tm = x_ref.shape[1]
    mgate = mod_ref[0][5:6]

    def emit(rows, acc):
        res = x_ref[0, rows, :] + mgate * acc
        o_ref[0, rows, :] = _rms(res, fg_ref[...]) if final else res

    fwin = min(SCATTER_FAST_WIN, cap)
    swin = min(SCATTER_WIN, cap)

    def window_start(ei, tb, win):
        start = st_ref[(bi * ne + ei) * nt_total + tb]
        return pl.multiple_of(jnp.minimum(jnp.bitwise_and(start, -BF16_ROWS), cap - win), BF16_ROWS)

    def fast_body(t, _):
        tb = tg * (tm // LANES) + t
        rows = pl.ds(pl.multiple_of(t * LANES, LANES), LANES)
        for ei in range(ne):
            p0 = window_start(ei, tb, fwin)
            ystk_ref[ei * fwin:(ei + 1) * fwin, :] = y_ref[ei, 0, pl.ds(p0, fwin), :]
        slot = selpos_ref[0, :, pl.ds(tb, 1), :]
        p0v = jnp.minimum(jnp.bitwise_and(stv_ref[0, tb], -BF16_ROWS), cap - fwin)
        riota = lax.broadcasted_iota(I32, (ne, fwin, LANES), 1)
        onehot_t = jnp.where(slot - p0v[:, :, None] == riota, 1.0, 0.0)
        acc = _tn(onehot_t.reshape(ne * fwin, LANES).astype(BF16), ystk_ref[...])
        emit(rows, acc)
        return 0

    def full_body(t, _):
        tb = tg * (tm // LANES) + t
        rows = pl.ds(pl.multiple_of(t * LANES, LANES), LANES)
        riota = lax.broadcasted_iota(I32, (swin, LANES), 0)
        acc = jnp.zeros((LANES, x_ref.shape[2]), F32)
        for ei in range(ne):
            p0 = window_start(ei, tb, swin)
            slot = selpos_ref[0, ei, pl.ds(tb, 1), :]
            onehot_t = jnp.where(slot - p0 == riota, 1.0, 0.0).astype(BF16)
            acc = acc + _tn(onehot_t, y_ref[ei, 0, pl.ds(p0, swin), :])
        emit(rows, acc)
        return 0

    most = most_ref[bi]

    @pl.when(most <= fwin - BF16_ROWS)
    def _():
        lax.fori_loop(0, tm // LANES, fast_body, 0)

    @pl.when(most > fwin - BF16_ROWS)
    def _():
        lax.fori_loop(0, tm // LANES, full_body, 0)


def _combine(st, most, selpos, stv, y, x, mod, cap, final_g=None):
    b, n, d = x.shape
    e = y.shape[0]
    tm = min(512, n)
    dh = d
    nt = n // LANES
    final = final_g is not None
    fg = final_g if final else jnp.ones((1, d), F32)
    grid_spec = pltpu.PrefetchScalarGridSpec(
        num_scalar_prefetch=2,
        grid=(b, d // dh, n // tm),
        in_specs=[pl.BlockSpec((1, e, nt, LANES), lambda i, k, j, p, q: (i, 0, 0, 0)),
                  pl.BlockSpec((1, nt, e, 1), lambda i, k, j, p, q: (i, 0, 0, 0)),
                  pl.BlockSpec((e, 1, cap, dh), lambda i, k, j, p, q: (0, i, 0, k)),
                  pl.BlockSpec((1, tm, dh), lambda i, k, j, p, q: (i, j, k)),
                  pl.BlockSpec((1, 6, dh), lambda i, k, j, p, q: (i, 0, k)),
                  pl.BlockSpec((1, dh), lambda i, k, j, p, q: (0, k))],
        out_specs=pl.BlockSpec((1, tm, dh), lambda i, k, j, p, q: (i, j, k)),
        scratch_shapes=[pltpu.VMEM((e * min(SCATTER_FAST_WIN, cap), dh), BF16)],
    )
    return pl.pallas_call(
        functools.partial(_combine_kernel, cap=cap, nt_total=nt, final=final),
        grid_spec=grid_spec,
        out_shape=jax.ShapeDtypeStruct((b, n, d), F32),
        name="moe_combine",
        compiler_params=_cparams("arbitrary", "arbitrary", "arbitrary"),
    )(st.reshape(-1), most.reshape(-1), selpos.reshape(b, e, nt, LANES), stv, y, x, mod, fg)


def _rot_cols(w):
    q = MLA_ROPE // 4
    return jnp.concatenate([-w[:, q:2 * q], w[:, :q], -w[:, 3 * q:], w[:, 2 * q:3 * q]], axis=1)


def _prep_even(w_in, wq_b, wkv_b):
    kpe = w_in[:, -MLA_ROPE:]
    win = jnp.concatenate([w_in, _rot_cols(kpe)], axis=1).astype(BF16)
    qh = wq_b.reshape(MLA_Q_RANK, MLA_HEADS, MLA_NOPE + MLA_ROPE)
    nope = [qh[:, h, :MLA_NOPE] for h in range(MLA_HEADS)]
    pe = [jnp.concatenate([qh[:, h, MLA_NOPE:], _rot_cols(qh[:, h, MLA_NOPE:])], axis=1) for h in range(MLA_HEADS)]
    wq = jnp.concatenate(nope + pe, axis=1).astype(BF16)
    kvh = wkv_b.reshape(MLA_KV_RANK, MLA_HEADS, MLA_NOPE + MLA_V)
    wkv = jnp.concatenate([kvh[:, h, :MLA_NOPE] for h in range(MLA_HEADS)]
                          + [kvh[:, h, MLA_NOPE:] for h in range(MLA_HEADS)], axis=1).astype(BF16)
    return win, wq, wkv


def _rope_tables(n):
    rows = n // GRID_W
    row = jnp.repeat(jnp.arange(rows, dtype=F32), GRID_W)
    col = jnp.tile(jnp.arange(GRID_W, dtype=F32), rows)
    half = MLA_ROPE // 2
    inv = 1.0 / (ROPE_THETA ** (jnp.arange(0, half, 2, dtype=F32) / half))
    ar = row[:, None] * inv[None, :]
    ac = col[:, None] * inv[None, :]
    ang = jnp.concatenate([ar, ar, ac, ac], axis=-1)
    z = jnp.zeros((n, LANES - MLA_ROPE), F32)
    return jnp.concatenate([jnp.cos(ang), z], axis=1), jnp.concatenate([jnp.sin(ang), z], axis=1)


def _moe(streams, wg, wu, wd, layer, final_g=None):
    routed = []
    for xx, mm, h2, aff_t in streams:
        b, n, d = xx.shape
        e = aff_t.shape[1]
        cap = EC_FACTOR * n // e
        selpos, gsel, st, stv, most = _topk(aff_t, cap)
        xs, gs = _gather(st, most, selpos, gsel, stv, h2, cap)
        routed.append((st, most, selpos, stv, cap, xs, gs))
    flat = lambda a: a.reshape(a.shape[0], -1, a.shape[3])
    ys = _ffn([flat(r[5]) for r in routed], [flat(r[6]) for r in routed], wg, wu, wd, layer)
    outs = []
    for si, ((xx, mm, _, _), (st, most, selpos, stv, cap, xs, gs), y) in enumerate(zip(streams, routed, ys)):
        outs.append(_combine(st, most, selpos, stv, y.reshape(xs.shape), xx, mm, cap,
                             final_g if si == 0 else None))
    return outs


def kernel(x, c, ctx, c_ctx, ada_w, ada_b, norm1_g, norm2_g, w_in, hg_lb, hg_norm_g, mla_qn_g, mla_wq_b,
           mla_kvn_g, mla_wkv_b, w_out, pool_w, pool_scale, router_w, exp_wg, exp_wu, exp_wd, final_g):
    b, n, d = x.shape
    nctx = ctx.shape[1]
    depth = ada_w.shape[0]
    n_even = w_in.shape[0]
    assert b + 1 <= 8 and n % LANES == 0 and nctx % LANES == 0 and n % GRID_W == 0

    cc = jnp.zeros((8, d), F32).at[:b].set(c).at[b].set(c_ctx)
    mods = _ada(cc, ada_w, ada_b)
    mod_lat = [mods[l, :b].reshape(b, 6, d) for l in range(depth)]
    mod_ctx = [jnp.broadcast_to(mods[l, b].reshape(1, 6, d), (b, 6, d)) for l in range(depth)]

    lb_all = jnp.cumsum(jax.nn.softmax(hg_lb.astype(F32), axis=0), axis=0)
    lb_all = lb_all - lb_all[:1]
    cos1, sin1 = _rope_tables(n)
    cos_c = jnp.concatenate([jnp.ones((nctx, MLA_ROPE), F32), jnp.zeros((nctx, LANES - MLA_ROPE), F32)], axis=1)
    sin_c = jnp.zeros((nctx, LANES), F32)

    last_reader = 2 * (n_even - 1)
    row = lambda v: v.reshape(1, -1)
    x_lat, x_ctx = x, ctx
    for l in range(depth):
        j = l // 2
        ctx_in = l <= last_reader
        ctx_out = l < last_reader
        g2, rt = row(norm2_g[l]), router_w[l].T
        if l % 2 == 0:
            win, wq, wkv = _prep_even(w_in[j], mla_wq_b[j], mla_wkv_b[j])
            lb = lb_all[j]
            lbp = jnp.stack([lb[0], 1.0 - lb[0], lb[1], 1.0 - lb[1]])
            common = (row(norm1_g[l]), win, row(mla_qn_g[j]), wq, row(mla_kvn_g[j]), wkv)
            hg_l, q_l, k_l, v_l = _in_proj(x_lat, mod_lat[l], *common, cos1, sin1)
            hg_c = None
            ks, vs = [k_l], [v_l]
            if ctx_in:
                hg_c, q_c, k_c, v_c = _in_proj(x_ctx, mod_ctx[l], *common, cos_c, sin_c)
                ks, vs = [k_c, k_l], [v_c, v_l]
            hgo_l, hgo_c = _hgrn(hg_l, hg_c, lbp, row(hg_norm_g[j]), ctx_out)
            mla_l = _attention(q_l, ks, vs)
            wo = w_out[j].astype(BF16)
            streams = [(mod_lat[l],) + tuple(_out_proj(hgo_l, mla_l, x_lat, mod_lat[l], wo, g2, rt))]
            if ctx_out:
                mla_c = _attention(q_c, [k_c], [v_c])
                streams.append((mod_ctx[l],) + tuple(_out_proj(hgo_c, mla_c, x_ctx, mod_ctx[l], wo, g2, rt)))
        else:
            pw = pool_w[j].astype(BF16)
            pool_args = (row(norm1_g[l]), pw, row(pool_scale[j]), g2, rt)
            streams = [(mod_lat[l],) + tuple(_pool(x_lat, mod_lat[l], *pool_args))]
            if ctx_out:
                streams.append((mod_ctx[l],) + tuple(_pool(x_ctx, mod_ctx[l], *pool_args)))
        outs = _moe([(xn, mm, h2, aff) for mm, xn, h2, aff in streams], exp_wg, exp_wu, exp_wd, l,
                    final_g=row(final_g) if l == depth - 1 else None)
        x_lat = outs[0]
        if ctx_out:
            x_ctx = outs[1]
    return x_lat
```

```python
import functools

import numpy as np
import jax
import jax.numpy as jnp
from jax import lax
from jax.experimental import pallas as pl
from jax.experimental.pallas import tpu as pltpu

F32 = jnp.float32
BF16 = jnp.bfloat16
I32 = jnp.int32

EPS = 1e-6
GRID_W = 64
ROPE_THETA = 10000.0
HG_HEADS = 4
HG_D = 128
MLA_HEADS = 4
MLA_Q_RANK = 256
MLA_KV_RANK = 128
MLA_NOPE = 128
MLA_ROPE = 64
MLA_V = 128
MLA_SCALE = (MLA_NOPE + MLA_ROPE) ** -0.5
POOL_WINDOWS = (2, 4, 8, 16)
EC_FACTOR = 2
HG_W = HG_HEADS * HG_D

LANES = 128
CHUNK = 64
HG_HEADS_PER_STEP = 1
HG_CHAINS = 16
POOL_HALO = 8
SUBLANES = 8
BF16_ROWS = 16
GATHER_WIN = LANES + BF16_ROWS
GATHER_FAST_WIN = 48
SCATTER_WIN = 256
SCATTER_FAST_WIN = 64
ATTN_Q_SUB = 512
ATTN_Q_TILE = 1024
LOG2E = 1.4426950408889634
VMEM_LIMIT_V7X = 56 * 1024 * 1024


def _cparams(*sem):
    return pltpu.CompilerParams(dimension_semantics=sem, vmem_limit_bytes=VMEM_LIMIT_V7X)


def _nt(a, b):
    return lax.dot_general(a, b, (((1,), (1,)), ((), ())), preferred_element_type=F32)


def _tn(a, b):
    return lax.dot_general(a, b, (((0,), (0,)), ((), ())), preferred_element_type=F32)


def _dot(a, b):
    return jnp.dot(a, b, preferred_element_type=F32)


def _sigmoid(x):
    return 1.0 / (1.0 + jnp.exp(-x))


def _rms(x, g):
    return x * lax.rsqrt(jnp.mean(x * x, axis=-1, keepdims=True) + EPS) * g


def _modulate(x, g, shift, scale):
    return _rms(x, g) * (1.0 + scale) + shift


def _split3(x):
    hi = x.astype(BF16)
    r1 = x - hi.astype(F32)
    mid = r1.astype(BF16)
    lo = (r1 - mid.astype(F32)).astype(BF16)
    return hi, mid, lo


def _ada_kernel(c_ref, w_ref, b_ref, o_ref):
    c = c_ref[...]
    s = (c * _sigmoid(c)).astype(BF16)
    o_ref[0] = _dot(s, w_ref[0].astype(BF16)) + b_ref[0]


def _ada(cc, ada_w, ada_b):
    depth, d, d6 = ada_w.shape
    tn = d6 // 4
    return pl.pallas_call(
        _ada_kernel,
        grid=(depth, d6 // tn),
        in_specs=[pl.BlockSpec((8, d), lambda l, j: (0, 0)),
                  pl.BlockSpec((1, d, tn), lambda l, j: (l, 0, j)),
                  pl.BlockSpec((1, 1, tn), lambda l, j: (l, 0, j))],
        out_specs=pl.BlockSpec((1, 8, tn), lambda l, j: (l, 0, j)),
        out_shape=jax.ShapeDtypeStruct((depth, 8, d6), F32),
        name="ada_mod",
        compiler_params=_cparams("arbitrary", "arbitrary"),
    )(cc, ada_w, ada_b.reshape(depth, 1, d6))


def _in_kernel(x_ref, mod_ref, g1_ref, win_ref, qng_ref, wq_ref, kvg_ref, wkv_ref, cs1_ref, cs2_ref,
               hg_ref, q_ref, k_ref, vt_ref):
    m = mod_ref[0]
    h = _modulate(x_ref[0], g1_ref[...], m[0:1], m[1:2]).astype(BF16)
    proj = _dot(h, win_ref[...])
    o = 5 * HG_W
    hg_ref[0] = proj[:, :o]
    qa = proj[:, o:o + MLA_Q_RANK]
    kva = proj[:, o + MLA_Q_RANK:o + MLA_Q_RANK + MLA_KV_RANK]
    kp = proj[:, o + MLA_Q_RANK + MLA_KV_RANK:]
    qup = _dot(_rms(qa, qng_ref[...]).astype(BF16), wq_ref[...])
    kvup = _dot(_rms(kva, kvg_ref[...]).astype(BF16), wkv_ref[...])
    cs1 = cs1_ref[...]
    cs2 = cs2_ref[...]
    half = LANES // 2
    kpe = (kp * cs1 + pltpu.roll(kp, half, axis=1) * cs2).astype(BF16)
    nn = MLA_HEADS * MLA_NOPE
    qs = MLA_SCALE * LOG2E
    for hd in range(MLA_HEADS):
        p = qup[:, nn + LANES * hd:nn + LANES * (hd + 1)]
        qpe = p * cs1 + pltpu.roll(p, half, axis=1) * cs2
        q_ref[0, :, 2 * LANES * hd:2 * LANES * hd + LANES] = (qup[:, LANES * hd:LANES * (hd + 1)] * qs).astype(BF16)
        q_ref[0, :, 2 * LANES * hd + LANES:2 * LANES * (hd + 1)] = (qpe * qs).astype(BF16)
        k_ref[0, :, 2 * LANES * hd:2 * LANES * hd + LANES] = kvup[:, LANES * hd:LANES * (hd + 1)].astype(BF16)
        k_ref[0, :, 2 * LANES * hd + LANES:2 * LANES * (hd + 1)] = kpe
    vt_ref[0] = kvup[:, nn:].T.astype(BF16)


def _in_proj(x, mod, g1, win, qng, wq, kvg, wkv, cs1, cs2):
    b, n, d = x.shape
    tm = min(512, n)
    wcols = win.shape[1]
    full = lambda shape: pl.BlockSpec(shape, lambda i, j: (0,) * len(shape))
    return pl.pallas_call(
        _in_kernel,
        grid=(b, n // tm),
        in_specs=[pl.BlockSpec((1, tm, d), lambda i, j: (i, j, 0)),
                  pl.BlockSpec((1, 6, d), lambda i, j: (i, 0, 0)),
                  full((1, d)), full((d, wcols)), full((1, MLA_Q_RANK)), full(wq.shape),
                  full((1, MLA_KV_RANK)), full(wkv.shape),
                  pl.BlockSpec((tm, LANES), lambda i, j: (j, 0)),
                  pl.BlockSpec((tm, LANES), lambda i, j: (j, 0))],
        out_specs=[pl.BlockSpec((1, tm, 5 * HG_W), lambda i, j: (i, j, 0)),
                   pl.BlockSpec((1, tm, 2 * LANES * MLA_HEADS), lambda i, j: (i, j, 0)),
                   pl.BlockSpec((1, tm, 2 * LANES * MLA_HEADS), lambda i, j: (i, j, 0)),
                   pl.BlockSpec((1, MLA_HEADS * MLA_V, tm), lambda i, j: (i, 0, j))],
        out_shape=[jax.ShapeDtypeStruct((b, n, 5 * HG_W), F32),
                   jax.ShapeDtypeStruct((b, n, 2 * LANES * MLA_HEADS), BF16),
                   jax.ShapeDtypeStruct((b, n, 2 * LANES * MLA_HEADS), BF16),
                   jax.ShapeDtypeStruct((b, MLA_HEADS * MLA_V, n), BF16)],
        name="in_proj",
        compiler_params=_cparams("arbitrary", "arbitrary"),
    )(x, mod, g1, win, qng, wq, kvg, wkv, cs1, cs2)


def _attn_kernel(*refs, nparts):
    q_ref = refs[0]
    k_refs = refs[1:1 + nparts]
    v_refs = refs[1 + nparts:1 + 2 * nparts]
    o_ref = refs[1 + 2 * nparts]
    tq = q_ref.shape[1]
    sub = min(ATTN_Q_SUB, tq)
    qk = lambda q0: [_nt(k[0], q_ref[0, q0:q0 + sub, :]) for k in k_refs]
    nxt = qk(0)
    for q0 in range(0, tq, sub):
        s = nxt
        if q0 + sub < tq:
            nxt = qk(q0 + sub)
        m = functools.reduce(jnp.maximum, [jnp.max(x, axis=0, keepdims=True) for x in s])
        p = [jnp.exp2(x - m) for x in s]
        den = functools.reduce(jnp.add, [jnp.sum(x, axis=0, keepdims=True) for x in p])
        acc = functools.reduce(jnp.add, [_dot(vt[0], x.astype(BF16)) for vt, x in zip(v_refs, p)])
        o_ref[0, q0:q0 + sub, :] = (acc / den).T.astype(o_ref.dtype)


def _attention(q, ks, vts):
    b, n, _ = q.shape
    tq = min(ATTN_Q_TILE, n)
    nparts = len(ks)
    in_specs = [pl.BlockSpec((1, tq, 2 * LANES), lambda i, h, j: (i, j, h))]
    in_specs += [pl.BlockSpec((1, k.shape[1], 2 * LANES), lambda i, h, j: (i, 0, h)) for k in ks]
    in_specs += [pl.BlockSpec((1, MLA_V, v.shape[2]), lambda i, h, j: (i, h, 0)) for v in vts]
    return pl.pallas_call(
        functools.partial(_attn_kernel, nparts=nparts),
        grid=(b, MLA_HEADS, n // tq),
        in_specs=in_specs,
        out_specs=pl.BlockSpec((1, tq, MLA_V), lambda i, h, j: (i, j, h)),
        out_shape=jax.ShapeDtypeStruct((b, n, MLA_HEADS * MLA_V), BF16),
        name="mla_attention",
        compiler_params=_cparams("arbitrary", "arbitrary", "arbitrary"),
    )(q, *ks, *vts)


_HG_HALVES = (1, 2, 4, 8, 16, 32)


def _hgrn_masks():
    t = np.arange(CHUNK)[:, None]
    s = np.arange(CHUNK)[None, :]
    masks_f = [(t == s)]
    for h in _HG_HALVES:
        masks_f.append((t // (2 * h) == s // (2 * h)) & (t % (2 * h) >= h) & (s % (2 * h) < h))
    masks_f = np.stack(masks_f).astype(np.float32)
    return masks_f, np.transpose(masks_f, (0, 2, 1))


def _hg_block_decays(f, fw):
    c = CHUNK
    pre, suf = [f], [jnp.ones_like(f)]
    h = 1
    while h < c:
        blk = 2 * h
        src_early = h - 1 if fw else h
        src_late = blk - 1 if fw else 0
        p = pre[-1]
        if blk <= SUBLANES:
            p3 = p.reshape(c // SUBLANES, SUBLANES, HG_D)
            r = lax.broadcasted_iota(I32, p3.shape, 1)
            tot_e = jnp.ones_like(p3)
            tot_l = jnp.ones_like(p3)
            for kb in range(SUBLANES // blk):
                inblk = jnp.right_shift(r, blk.bit_length() - 1) == kb
                tot_e = jnp.where(inblk, p3[:, kb * blk + src_early:kb * blk + src_early + 1, :], tot_e)
                tot_l = jnp.where(inblk, p3[:, kb * blk + src_late:kb * blk + src_late + 1, :], tot_l)
        else:
            p3 = p.reshape(c // blk, blk, HG_D)
            r = lax.broadcasted_iota(I32, p3.shape, 1)
            tot_e = jnp.broadcast_to(p3[:, src_early:src_early + 1, :], p3.shape)
            tot_l = jnp.broadcast_to(p3[:, src_late:src_late + 1, :], p3.shape)
        in_first = jnp.bitwise_and(r, blk - 1) < h
        late = jnp.logical_not(in_first) if fw else in_first
        s3 = suf[-1].reshape(p3.shape)
        pre.append(jnp.where(late, p3 * tot_e, p3).reshape(c, HG_D))
        suf.append(jnp.where(late, s3, s3 * tot_l).reshape(c, HG_D))
        h = blk
    return pre, suf


def _hg_chunk(q, z, v, lb, onemlb, masks, st, fw, want_out):
    t = jnp.exp(-jnp.abs(z))
    r = 1.0 / (1.0 + t)
    pos = z >= 0.0
    f = lb + onemlb * jnp.where(pos, r, t * r)
    k = onemlb * jnp.where(pos, t * r, r)
    pre, suf = _hg_block_decays(f, fw)
    vb = v.astype(BF16)
    whole = pre[-1]
    kdl = (k * suf[-1]).astype(BF16)
    last = CHUNK - 1 if fw else 0
    dec = whole[last:last + 1]
    out = None
    if want_out:
        kb = k.astype(BF16)
        att = masks[0] * _nt(q.astype(BF16), kb)
        for i in range(len(_HG_HALVES)):
            a = (q * pre[i]).astype(BF16)
            bm = kb if i == 0 else (k * suf[i]).astype(BF16)
            att = att + masks[1 + i] * _nt(a, bm)
        out = _dot(att.astype(BF16), vb) + _nt((q * whole).astype(BF16), st.astype(BF16))
    st_new = st * dec + _tn(vb, kdl)
    return out, st_new


def _hgrn_kernel(*refs, has_ctx, ctx_out):
    it = iter(refs)
    q_ref, zf_ref, zb_ref, v_ref, g_ref = (next(it) for _ in range(5))
    if has_ctx:
        qc_ref, zfc_ref, zbc_ref, vc_ref, gc_ref = (next(it) for _ in range(5))
    lb_ref, gn_ref, mkf_ref, mkb_ref = (next(it) for _ in range(4))
    o_ref = next(it)
    oc_ref = next(it) if ctx_out else None
    of_s, ob_s = next(it), next(it)
    if ctx_out:
        ocf_s, ocb_s = next(it), next(it)

    lbp = lb_ref[...]
    masks = (mkf_ref[...], mkb_ref[...])
    zrefs = (zf_ref, zb_ref)

    def seq_pass(q_r, z_rs, v_r, o_ss, n, states, want_out):
        nc = n // CHUNK

        def body(ci, carry):
            new = []
            for hd in range(HG_HEADS_PER_STEP):
                cols = slice(hd * HG_D, (hd + 1) * HG_D)
                for d in range(2):
                    cidx = ci if d == 0 else nc - 1 - ci
                    off = pl.multiple_of(cidx * CHUNK, CHUNK)
                    rows = pl.ds(off, CHUNK)
                    qv = q_r[0, rows, cols] if want_out else None
                    out, st = _hg_chunk(qv, z_rs[d][0, rows, cols], v_r[0, rows, cols],
                                        lbp[2 * d:2 * d + 1, cols], lbp[2 * d + 1:2 * d + 2, cols],
                                        masks[d], carry[2 * hd + d], d == 0, want_out)
                    if want_out:
                        o_ss[d][rows, cols] = out
                    new.append(st)
            return tuple(new)

        return lax.fori_loop(0, nc, body, states, unroll=HG_CHAINS // (2 * HG_HEADS_PER_STEP))

    def readout(o_ss, g_r, out_r, n):
        tr = min(512, n)

        def body(i, _):
            rows = pl.ds(pl.multiple_of(i * tr, tr), tr)
            for hd in range(HG_HEADS_PER_STEP):
                cols = slice(hd * HG_D, (hd + 1) * HG_D)
                o = o_ss[0][rows, cols] + o_ss[1][rows, cols]
                g = g_r[0, rows, cols]
                out_r[0, rows, cols] = (_rms(o, gn_ref[...]) * (g * _sigmoid(g))).astype(out_r.dtype)
            return 0

        lax.fori_loop(0, n // tr, body, 0)

    zero = jnp.zeros((HG_D, HG_D), F32)
    states = (zero,) * (2 * HG_HEADS_PER_STEP)
    if has_ctx:
        nctx = vc_ref.shape[1]
        states = seq_pass(qc_ref, (zfc_ref, zbc_ref), vc_ref, (ocf_s, ocb_s) if ctx_out else None,
                          nctx, states, ctx_out)
        if ctx_out:
            readout((ocf_s, ocb_s), gc_ref, oc_ref, nctx)
    n = v_ref.shape[1]
    seq_pass(q_ref, zrefs, v_ref, (of_s, ob_s), n, states, True)
    readout((of_s, ob_s), g_ref, o_ref, n)


def _hgrn(hg, hg_ctx, lbp, gn, ctx_out):
    b, n, _ = hg.shape
    has_ctx = hg_ctx is not None
    consts = [jnp.asarray(a, F32) for a in _hgrn_masks()]

    hw = HG_HEADS_PER_STEP * HG_D
    groups = HG_HEADS // HG_HEADS_PER_STEP

    def part(nn, k):
        return pl.BlockSpec((1, nn, hw), lambda i, h, k=k: (i, 0, k * groups + h))

    in_specs = [part(n, k) for k in range(5)]
    args = [hg] * 5
    if has_ctx:
        nctx = hg_ctx.shape[1]
        in_specs += [part(nctx, k) for k in range(5)]
        args += [hg_ctx] * 5
    in_specs += [pl.BlockSpec((4, hw), lambda i, h: (0, h)),
                 pl.BlockSpec((1, HG_D), lambda i, h: (0, 0))]
    in_specs += [pl.BlockSpec(a.shape, lambda i, h, nd=a.ndim: (0,) * nd) for a in consts]
    args += [lbp, gn] + consts
    out_specs = [pl.BlockSpec((1, n, hw), lambda i, h: (i, 0, h))]
    out_shape = [jax.ShapeDtypeStruct((b, n, HG_W), BF16)]
    scratch = [pltpu.VMEM((n, hw), F32), pltpu.VMEM((n, hw), F32)]
    if ctx_out:
        out_specs.append(pl.BlockSpec((1, nctx, hw), lambda i, h: (i, 0, h)))
        out_shape.append(jax.ShapeDtypeStruct((b, nctx, HG_W), BF16))
        scratch += [pltpu.VMEM((nctx, hw), F32), pltpu.VMEM((nctx, hw), F32)]
    res = pl.pallas_call(
        functools.partial(_hgrn_kernel, has_ctx=has_ctx, ctx_out=ctx_out),
        grid=(b, groups),
        in_specs=in_specs, out_specs=out_specs, out_shape=out_shape, scratch_shapes=scratch,
        name="hgrn_scan",
        compiler_params=_cparams("arbitrary", "arbitrary"),
    )(*args)
    return (res[0], res[1]) if ctx_out else (res[0], None)


def _route_tile(x, m, g2, rt):
    h = _modulate(x, g2, m[3:4], m[4:5])
    hb = h.astype(BF16)
    hlo = (h - hb.astype(F32)).astype(BF16)
    rhi = rt.astype(BF16)
    rlo = (rt - rhi.astype(F32)).astype(BF16)
    logits = _nt(rhi, hb) + (_nt(rhi, hlo) + _nt(rlo, hb))
    ex = jnp.exp(logits - jnp.max(logits, axis=0, keepdims=True))
    return hb, ex / jnp.sum(ex, axis=0, keepdims=True)


def _mixer_out_specs(b, n, d, e, tm):
    specs = [pl.BlockSpec((1, tm, d), lambda i, j: (i, j, 0)),
             pl.BlockSpec((1, tm, d), lambda i, j: (i, j, 0)),
             pl.BlockSpec((1, e, tm), lambda i, j: (i, 0, j))]
    shapes = [jax.ShapeDtypeStruct((b, n, d), F32),
              jax.ShapeDtypeStruct((b, n, d), BF16),
              jax.ShapeDtypeStruct((b, e, n), F32)]
    return specs, shapes


def _out_kernel(hg_ref, mla_ref, x_ref, mod_ref, w_ref, g2_ref, rt_ref, o_ref, h_ref, aff_ref):
    w = w_ref[...]
    m = mod_ref[0]
    y = _dot(hg_ref[0], w[:HG_W]) + _dot(mla_ref[0], w[HG_W:])
    xn = x_ref[0] + m[2:3] * y
    o_ref[0] = xn
    h_ref[0], aff_ref[0] = _route_tile(xn, m, g2_ref[...], rt_ref[...])


def _out_proj(hgo, mla, x, mod, w, g2, rt):
    b, n, d = x.shape
    e = rt.shape[0]
    tm = min(512, n)
    out_specs, out_shape = _mixer_out_specs(b, n, d, e, tm)
    return pl.pallas_call(
        _out_kernel,
        grid=(b, n // tm),
        in_specs=[pl.BlockSpec((1, tm, HG_W), lambda i, j: (i, j, 0)),
                  pl.BlockSpec((1, tm, MLA_HEADS * MLA_V), lambda i, j: (i, j, 0)),
                  pl.BlockSpec((1, tm, d), lambda i, j: (i, j, 0)),
                  pl.BlockSpec((1, 6, d), lambda i, j: (i, 0, 0)),
                  pl.BlockSpec(w.shape, lambda i, j: (0, 0)),
                  pl.BlockSpec((1, d), lambda i, j: (0, 0)),
                  pl.BlockSpec((e, d), lambda i, j: (0, 0))],
        out_specs=out_specs, out_shape=out_shape,
        name="out_proj",
        compiler_params=_cparams("arbitrary", "arbitrary"),
    )(hgo, mla, x, mod, w, g2, rt)


def _pool_kernel(x_ref, prev_ref, next_ref, mod_ref, g1_ref, pw_ref, ps_ref, g2_ref, rt_ref,
                 o_ref, h_ref, aff_ref, hs_ref, *, n):
    tm = x_ref.shape[1]
    d = x_ref.shape[2]
    gw = d // len(POOL_WINDOWS)
    m = mod_ref[0]
    base = pl.program_id(1) * tm
    xe = jnp.concatenate([prev_ref[0], x_ref[0], next_ref[0]], axis=0)
    he = _modulate(xe, g1_ref[...], m[0:1], m[1:2])
    gidx = base - POOL_HALO + lax.broadcasted_iota(I32, (tm + 2 * POOL_HALO, 1), 0)
    hs_ref[...] = jnp.where((gidx >= 0) & (gidx < n), he, 0.0)
    t = base + lax.broadcasted_iota(I32, (tm, 1), 0)
    ps = ps_ref[...]
    for gi, w in enumerate(POOL_WINDOWS):
        cols = slice(gi * gw, (gi + 1) * gw)
        acc = hs_ref[pl.ds(POOL_HALO - w // 2, tm), cols]
        for j in range(1 - w // 2, w // 2):
            acc = acc + hs_ref[pl.ds(POOL_HALO + j, tm), cols]
        lo = jnp.clip(t - w // 2, 0, n - 1)
        hi = jnp.clip(t + w // 2 - 1, 0, n - 1)
        pooled = acc / (hi - lo + 1).astype(F32) - hs_ref[pl.ds(POOL_HALO, tm), cols]
        y = _dot(pooled.astype(BF16), pw_ref[gi]) * ps[:, cols]
        o_ref[0, :, cols] = x_ref[0, :, cols] + m[2:3, cols] * y
    h_ref[0], aff_ref[0] = _route_tile(o_ref[0], m, g2_ref[...], rt_ref[...])


def _pool(x, mod, g1, pw, ps, g2, rt):
    b, n, d = x.shape
    e = rt.shape[0]
    tm = min(512, n)
    nb = tm // POOL_HALO
    last = n // POOL_HALO - 1
    out_specs, out_shape = _mixer_out_specs(b, n, d, e, tm)
    return pl.pallas_call(
        functools.partial(_pool_kernel, n=n),
        grid=(b, n // tm),
        in_specs=[pl.BlockSpec((1, tm, d), lambda i, j: (i, j, 0)),
                  pl.BlockSpec((1, POOL_HALO, d), lambda i, j: (i, jnp.maximum(j * nb - 1, 0), 0)),
                  pl.BlockSpec((1, POOL_HALO, d), lambda i, j: (i, jnp.minimum((j + 1) * nb, last), 0)),
                  pl.BlockSpec((1, 6, d), lambda i, j: (i, 0, 0)),
                  pl.BlockSpec((1, d), lambda i, j: (0, 0)),
                  pl.BlockSpec(pw.shape, lambda i, j: (0, 0, 0)),
                  pl.BlockSpec((1, d), lambda i, j: (0, 0)),
                  pl.BlockSpec((1, d), lambda i, j: (0, 0)),
                  pl.BlockSpec((e, d), lambda i, j: (0, 0))],
        out_specs=out_specs, out_shape=out_shape,
        scratch_shapes=[pltpu.VMEM((tm + 2 * POOL_HALO, d), F32)],
        name="pool_mixer",
        compiler_params=_cparams("arbitrary", "arbitrary"),
    )(x, x, x, mod, g1, pw, ps, g2, rt)


def _topk_kernel(aff_ref, selpos_ref, gsel_ref, st_ref, stv_ref, most_ref, *, cap):
    aff = aff_ref[0]
    e, n = aff.shape
    nt = n // LANES
    capf = float(cap)

    def bit_step(i, tau):
        cand = tau | jnp.left_shift(jnp.int32(1), 30 - i)
        cnt = jnp.sum(jnp.where(aff >= pltpu.bitcast(cand, F32), 1.0, 0.0), axis=1, keepdims=True)
        return jnp.where(cnt >= capf, cand, tau)

    tau = lax.fori_loop(0, 31, bit_step, jnp.zeros((e, 1), I32))
    gt = aff >= pltpu.bitcast(tau + 1, F32)
    eq = (aff >= pltpu.bitcast(tau, F32)) & jnp.logical_not(gt)
    need = capf - jnp.sum(jnp.where(gt, 1.0, 0.0), axis=1, keepdims=True)

    r = lax.broadcasted_iota(I32, (LANES, LANES), 0)
    c = lax.broadcasted_iota(I32, (LANES, LANES), 1)
    upper = jnp.where(r < c, 1.0, 0.0).astype(BF16)

    def excl_prefix(mask_f):
        off = jnp.zeros((e, 1), F32)
        outs, starts = [], []
        for tb in range(nt):
            blk = mask_f[:, tb * LANES:(tb + 1) * LANES]
            outs.append(_dot(blk.astype(BF16), upper) + off)
            starts.append(off)
            off = off + jnp.sum(blk, axis=1, keepdims=True)
        return jnp.concatenate(outs, axis=1), starts

    eq_rank, _ = excl_prefix(jnp.where(eq, 1.0, 0.0))
    sel = gt | (eq & (eq_rank < need))
    pos, starts = excl_prefix(jnp.where(sel, 1.0, 0.0))
    selpos = jnp.where(sel, pos, -1.0)
    selpos_ref[0] = selpos.astype(I32)

    gsel_ref[0] = jnp.where(sel, aff, 0.0)

    lane = lax.broadcasted_iota(I32, (e, nt), 1)
    st = jnp.zeros((e, nt), F32)
    most = jnp.zeros((e, 1), F32)
    for tb in range(nt):
        st = jnp.where(lane == tb, starts[tb], st)
        stv_ref[0, tb] = starts[tb].astype(I32)
        nxt = starts[tb + 1] if tb + 1 < nt else capf
        most = jnp.maximum(most, nxt - starts[tb])
    st_ref[0] = st.astype(I32)
    most_ref[0] = jnp.max(most, axis=0, keepdims=True).astype(I32)


def _topk(aff_t, cap):
    b, e, n = aff_t.shape
    nt = n // LANES
    return pl.pallas_call(
        functools.partial(_topk_kernel, cap=cap),
        grid=(b,),
        in_specs=[pl.BlockSpec((1, e, n), lambda i: (i, 0, 0))],
        out_specs=[pl.BlockSpec((1, e, n), lambda i: (i, 0, 0)),
                   pl.BlockSpec((1, e, n), lambda i: (i, 0, 0)),
                   pl.BlockSpec((1, e, nt), lambda i: (i, 0, 0)),
                   pl.BlockSpec((1, nt, e, 1), lambda i: (i, 0, 0, 0)),
                   pl.BlockSpec((1, 1, 1), lambda i: (i, 0, 0))],
        out_shape=[jax.ShapeDtypeStruct((b, e, n), I32),
                   jax.ShapeDtypeStruct((b, e, n), F32),
                   jax.ShapeDtypeStruct((b, e, nt), I32),
                   jax.ShapeDtypeStruct((b, nt, e, 1), I32),
                   jax.ShapeDtypeStruct((b, 1, 1), I32)],
        name="moe_topk",
        compiler_params=_cparams("arbitrary"),
    )(aff_t)


def _gather_kernel(st_ref, most_ref, selpos_ref, gsel_ref, stv_ref, h_ref, xs_ref, gs_ref, *, cap):
    bi = pl.program_id(0)
    ne = xs_ref.shape[0]
    nt = h_ref.shape[1] // LANES
    first_cols = pl.program_id(1) == 0
    xs_ref[...] = jnp.zeros_like(xs_ref)

    @pl.when(first_cols)
    def _():
        gs_ref[...] = jnp.zeros_like(gs_ref)

    keep_gate = jnp.where(first_cols, 1.0, 0.0)

    def run(win, unroll):
        riota = lax.broadcasted_iota(I32, (ne, win, LANES), 1)

        def body(tb, _):
            rows = pl.ds(pl.multiple_of(tb * LANES, LANES), LANES)
            slot = selpos_ref[0, :, pl.ds(tb, 1), :]
            p0v = jnp.minimum(jnp.bitwise_and(stv_ref[0, tb], -BF16_ROWS), cap - win)
            onehot = jnp.where(slot - p0v[:, :, None] == riota, 1.0, 0.0)
            res = _dot(onehot.reshape(ne * win, LANES).astype(BF16), h_ref[0, rows, :])
            gate = jnp.sum(onehot * gsel_ref[0, :, pl.ds(tb, 1), :], axis=-1, keepdims=True) * keep_gate
            for ei in range(ne):
                start = st_ref[(bi * ne + ei) * nt + tb]
                p0 = pl.multiple_of(jnp.minimum(jnp.bitwise_and(start, -BF16_ROWS), cap - win), BF16_ROWS)
                xs_ref[ei, 0, pl.ds(p0, win), :] += res[ei * win:(ei + 1) * win].astype(xs_ref.dtype)
                gs_ref[ei, 0, pl.ds(p0, win), :] += gate[ei]
            return 0

        lax.fori_loop(0, nt, body, 0, unroll=unroll)

    fast = min(GATHER_FAST_WIN, cap)
    full = min(GATHER_WIN, cap)
    most = most_ref[bi]

    @pl.when(most <= fast - BF16_ROWS)
    def _():
        run(fast, min(2, nt))

    @pl.when(most > fast - BF16_ROWS)
    def _():
        run(full, 1)


def _gather(st, most, selpos, gsel, stv, h2, cap):
    b, n, d = h2.shape
    e = selpos.shape[1]
    nt = n // LANES
    dh = d // 2
    grid_spec = pltpu.PrefetchScalarGridSpec(
        num_scalar_prefetch=2,
        grid=(b, d // dh),
        in_specs=[pl.BlockSpec((1, e, nt, LANES), lambda i, k, p, q: (i, 0, 0, 0)),
                  pl.BlockSpec((1, e, nt, LANES), lambda i, k, p, q: (i, 0, 0, 0)),
                  pl.BlockSpec((1, nt, e, 1), lambda i, k, p, q: (i, 0, 0, 0)),
                  pl.BlockSpec((1, n, dh), lambda i, k, p, q: (i, 0, k))],
        out_specs=[pl.BlockSpec((e, 1, cap, dh), lambda i, k, p, q: (0, i, 0, k)),
                   pl.BlockSpec((e, 1, cap, 1), lambda i, k, p, q: (0, i, 0, 0))],
    )
    return pl.pallas_call(
        functools.partial(_gather_kernel, cap=cap),
        grid_spec=grid_spec,
        out_shape=[jax.ShapeDtypeStruct((e, b, cap, d), BF16),
                   jax.ShapeDtypeStruct((e, b, cap, 1), F32)],
        name="moe_gather",
        compiler_params=_cparams("arbitrary", "arbitrary"),
    )(st.reshape(-1), most.reshape(-1), selpos.reshape(b, e, nt, LANES), gsel.reshape(b, e, nt, LANES), stv, h2)


def _ffn_kernel(*refs, ngroups):
    xs = refs[:ngroups]
    gs = refs[ngroups:2 * ngroups]
    wg_ref, wu_ref, wd_ref = refs[2 * ngroups:2 * ngroups + 3]
    ys = refs[2 * ngroups + 3:3 * ngroups + 3]
    accs = refs[3 * ngroups + 3:]
    f = pl.program_id(1)

    @pl.when(f == 0)
    def _():
        for acc_ref in accs:
            acc_ref[...] = jnp.zeros_like(acc_ref)

    wg = wu = wd = None
    for x_ref, acc_ref in zip(xs, accs):
        rows = x_ref.shape[1]
        tr = min(512, rows)
        for r0 in range(0, rows, tr):
            x = x_ref[0, r0:r0 + tr, :]
            wg = wg_ref[0, 0].astype(BF16) if wg is None else wg
            g = _dot(x, wg)
            wu = wu_ref[0, 0].astype(BF16) if wu is None else wu
            u = _dot(x, wu)
            wd = wd_ref[0, 0].astype(BF16) if wd is None else wd
            acc_ref[r0:r0 + tr, :] += _dot((g * _sigmoid(g) * u).astype(BF16), wd)

    @pl.when(f == pl.num_programs(1) - 1)
    def _():
        for g_ref, y_ref, acc_ref in zip(gs, ys, accs):
            y_ref[0] = (acc_ref[...] * g_ref[0]).astype(y_ref.dtype)


def _ffn(xs_list, gs_list, wg, wu, wd, layer):
    _, e, d, ff = wg.shape
    tf = min(512, ff)
    ng = len(xs_list)
    in_specs = [pl.BlockSpec((1, x.shape[1], d), lambda i, j: (i, 0, 0)) for x in xs_list]
    in_specs += [pl.BlockSpec((1, x.shape[1], 1), lambda i, j: (i, 0, 0)) for x in xs_list]
    in_specs += [pl.BlockSpec((1, 1, d, tf), lambda i, j: (layer, i, 0, j)),
                 pl.BlockSpec((1, 1, d, tf), lambda i, j: (layer, i, 0, j)),
                 pl.BlockSpec((1, 1, tf, d), lambda i, j: (layer, i, j, 0))]
    return pl.pallas_call(
        functools.partial(_ffn_kernel, ngroups=ng),
        grid=(e, ff // tf),
        in_specs=in_specs,
        out_specs=[pl.BlockSpec((1, x.shape[1], d), lambda i, j: (i, 0, 0)) for x in xs_list],
        out_shape=[jax.ShapeDtypeStruct(x.shape, BF16) for x in xs_list],
        scratch_shapes=[pltpu.VMEM((x.shape[1], d), F32) for x in xs_list],
        name="moe_ffn",
        compiler_params=_cparams("arbitrary", "arbitrary"),
    )(*xs_list, *gs_list, wg, wu, wd)


def _combine_kernel(st_ref, most_ref, selpos_ref, stv_ref, y_ref, x_ref, mod_ref, fg_ref, o_ref, ystk_ref, *,
                    cap, nt_total, final):
    bi = pl.program_id(0)
    tg = pl.program_id(2)
    ne = y_ref.shape[0]
    tm = x_ref.shape[1]
    mgate = mod_ref[0][5:6]

    def emit(rows, acc):
        res = x_ref[0, rows, :] + mgate * acc
        o_ref[0, rows, :] = _rms(res, fg_ref[...]) if final else res

    fwin = min(SCATTER_FAST_WIN, cap)
    swin = min(SCATTER_WIN, cap)

    def window_start(ei, tb, win):
        start = st_ref[(bi * ne + ei) * nt_total + tb]
        return pl.multiple_of(jnp.minimum(jnp.bitwise_and(start, -BF16_ROWS), cap - win), BF16_ROWS)

    def fast_body(t, _):
        tb = tg * (tm // LANES) + t
        rows = pl.ds(pl.multiple_of(t * LANES, LANES), LANES)
        for ei in range(ne):
            p0 = window_start(ei, tb, fwin)
            ystk_ref[ei * fwin:(ei + 1) * fwin, :] = y_ref[ei, 0, pl.ds(p0, fwin), :]
        slot = selpos_ref[0, :, pl.ds(tb, 1), :]
        p0v = jnp.minimum(jnp.bitwise_and(stv_ref[0, tb], -BF16_ROWS), cap - fwin)
        riota = lax.broadcasted_iota(I32, (ne, fwin, LANES), 1)
        onehot_t = jnp.where(slot - p0v[:, :, None] == riota, 1.0, 0.0)
        acc = _tn(onehot_t.reshape(ne * fwin, LANES).astype(BF16), ystk_ref[...])
        emit(rows, acc)
        return 0

    def full_body(t, _):
        tb = tg * (tm // LANES) + t
        rows = pl.ds(pl.multiple_of(t * LANES, LANES), LANES)
        riota = lax.broadcasted_iota(I32, (swin, LANES), 0)
        acc = jnp.zeros((LANES, x_ref.shape[2]), F32)
        for ei in range(ne):
            p0 = window_start(ei, tb, swin)
            slot = selpos_ref[0, ei, pl.ds(tb, 1), :]
            onehot_t = jnp.where(slot - p0 == riota, 1.0, 0.0).astype(BF16)
            acc = acc + _tn(onehot_t, y_ref[ei, 0, pl.ds(p0, swin), :])
        emit(rows, acc)
        return 0

    most = most_ref[bi]

    @pl.when(most <= fwin - BF16_ROWS)
    def _():
        lax.fori_loop(0, tm // LANES, fast_body, 0)

    @pl.when(most > fwin - BF16_ROWS)
    def _():
        lax.fori_loop(0, tm // LANES, full_body, 0)


def _combine(st, most, selpos, stv, y, x, mod, cap, final_g=None):
    b, n, d = x.shape
    e = y.shape[0]
    tm = min(512, n)
    dh = d
    nt = n // LANES
    final = final_g is not None
    fg = final_g if final else jnp.ones((1, d), F32)
    grid_spec = pltpu.PrefetchScalarGridSpec(
        num_scalar_prefetch=2,
        grid=(b, d // dh, n // tm),
        in_specs=[pl.BlockSpec((1, e, nt, LANES), lambda i, k, j, p, q: (i, 0, 0, 0)),
                  pl.BlockSpec((1, nt, e, 1), lambda i, k, j, p, q: (i, 0, 0, 0)),
                  pl.BlockSpec((e, 1, cap, dh), lambda i, k, j, p, q: (0, i, 0, k)),
                  pl.BlockSpec((1, tm, dh), lambda i, k, j, p, q: (i, j, k)),
                  pl.BlockSpec((1, 6, dh), lambda i, k, j, p, q: (i, 0, k)),
                  pl.BlockSpec((1, dh), lambda i, k, j, p, q: (0, k))],
        out_specs=pl.BlockSpec((1, tm, dh), lambda i, k, j, p, q: (i, j, k)),
        scratch_shapes=[pltpu.VMEM((e * min(SCATTER_FAST_WIN, cap), dh), BF16)],
    )
    return pl.pallas_call(
        functools.partial(_combine_kernel, cap=cap, nt_total=nt, final=final),
        grid_spec=grid_spec,
        out_shape=jax.ShapeDtypeStruct((b, n, d), F32),
        name="moe_combine",
        compiler_params=_cparams("arbitrary", "arbitrary", "arbitrary"),
    )(st.reshape(-1), most.reshape(-1), selpos.reshape(b, e, nt, LANES), stv, y, x, mod, fg)


def _rot_cols(w):
    q = MLA_ROPE // 4
    return jnp.concatenate([-w[:, q:2 * q], w[:, :q], -w[:, 3 * q:], w[:, 2 * q:3 * q]], axis=1)


def _prep_even(w_in, wq_b, wkv_b):
    kpe = w_in[:, -MLA_ROPE:]
    win = jnp.concatenate([w_in, _rot_cols(kpe)], axis=1).astype(BF16)
    qh = wq_b.reshape(MLA_Q_RANK, MLA_HEADS, MLA_NOPE + MLA_ROPE)
    nope = [qh[:, h, :MLA_NOPE] for h in range(MLA_HEADS)]
    pe = [jnp.concatenate([qh[:, h, MLA_NOPE:], _rot_cols(qh[:, h, MLA_NOPE:])], axis=1) for h in range(MLA_HEADS)]
    wq = jnp.concatenate(nope + pe, axis=1).astype(BF16)
    kvh = wkv_b.reshape(MLA_KV_RANK, MLA_HEADS, MLA_NOPE + MLA_V)
    wkv = jnp.concatenate([kvh[:, h, :MLA_NOPE] for h in range(MLA_HEADS)]
                          + [kvh[:, h, MLA_NOPE:] for h in range(MLA_HEADS)], axis=1).astype(BF16)
    return win, wq, wkv


def _rope_tables(n):
    rows = n // GRID_W
    row = jnp.repeat(jnp.arange(rows, dtype=F32), GRID_W)
    col = jnp.tile(jnp.arange(GRID_W, dtype=F32), rows)
    half = MLA_ROPE // 2
    inv = 1.0 / (ROPE_THETA ** (jnp.arange(0, half, 2, dtype=F32) / half))
    ar = row[:, None] * inv[None, :]
    ac = col[:, None] * inv[None, :]
    ang = jnp.concatenate([ar, ar, ac, ac], axis=-1)
    z = jnp.zeros((n, LANES - MLA_ROPE), F32)
    return jnp.concatenate([jnp.cos(ang), z], axis=1), jnp.concatenate([jnp.sin(ang), z], axis=1)


def _moe(streams, wg, wu, wd, layer, final_g=None):
    routed = []
    for xx, mm, h2, aff_t in streams:
        b, n, d = xx.shape
        e = aff_t.shape[1]
        cap = EC_FACTOR * n // e
        selpos, gsel, st, stv, most = _topk(aff_t, cap)
        xs, gs = _gather(st, most, selpos, gsel, stv, h2, cap)
        routed.append((st, most, selpos, stv, cap, xs, gs))
    flat = lambda a: a.reshape(a.shape[0], -1, a.shape[3])
    ys = _ffn([flat(r[5]) for r in routed], [flat(r[6]) for r in routed], wg, wu, wd, layer)
    outs = []
    for si, ((xx, mm, _, _), (st, most, selpos, stv, cap, xs, gs), y) in enumerate(zip(streams, routed, ys)):
        outs.append(_combine(st, most, selpos, stv, y.reshape(xs.shape), xx, mm, cap,
                             final_g if si == 0 else None))
    return outs


def kernel(x, c, ctx, c_ctx, ada_w, ada_b, norm1_g, norm2_g, w_in, hg_lb, hg_norm_g, mla_qn_g, mla_wq_b,
           mla_kvn_g, mla_wkv_b, w_out, pool_w, pool_scale, router_w, exp_wg, exp_wu, exp_wd, final_g):
    b, n, d = x.shape
    nctx = ctx.shape[1]
    depth = ada_w.shape[0]
    n_even = w_in.shape[0]
    assert b + 1 <= 8 and n % LANES == 0 and nctx % LANES == 0 and n % GRID_W == 0

    cc = jnp.zeros((8, d), F32).at[:b].set(c).at[b].set(c_ctx)
    mods = _ada(cc, ada_w, ada_b)
    mod_lat = [mods[l, :b].reshape(b, 6, d) for l in range(depth)]
    mod_ctx = [jnp.broadcast_to(mods[l, b].reshape(1, 6, d), (b, 6, d)) for l in range(depth)]

    lb_all = jnp.cumsum(jax.nn.softmax(hg_lb.astype(F32), axis=0), axis=0)
    lb_all = lb_all - lb_all[:1]
    cos1, sin1 = _rope_tables(n)
    cos_c = jnp.concatenate([jnp.ones((nctx, MLA_ROPE), F32), jnp.zeros((nctx, LANES - MLA_ROPE), F32)], axis=1)
    sin_c = jnp.zeros((nctx, LANES), F32)

    last_reader = 2 * (n_even - 1)
    row = lambda v: v.reshape(1, -1)
    x_lat, x_ctx = x, ctx
    for l in range(depth):
        j = l // 2
        ctx_in = l <= last_reader
        ctx_out = l < last_reader
        g2, rt = row(norm2_g[l]), router_w[l].T
        if l % 2 == 0:
            win, wq, wkv = _prep_even(w_in[j], mla_wq_b[j], mla_wkv_b[j])
            lb = lb_all[j]
            lbp = jnp.stack([lb[0], 1.0 - lb[0], lb[1], 1.0 - lb[1]])
            common = (row(norm1_g[l]), win, row(mla_qn_g[j]), wq, row(mla_kvn_g[j]), wkv)
            hg_l, q_l, k_l, v_l = _in_proj(x_lat, mod_lat[l], *common, cos1, sin1)
            hg_c = None
            ks, vs = [k_l], [v_l]
            if ctx_in:
                hg_c, q_c, k_c, v_c = _in_proj(x_ctx, mod_ctx[l], *common, cos_c, sin_c)
                ks, vs = [k_c, k_l], [v_c, v_l]
            hgo_l, hgo_c = _hgrn(hg_l, hg_c, lbp, row(hg_norm_g[j]), ctx_out)
            mla_l = _attention(q_l, ks, vs)
            wo = w_out[j].astype(BF16)
            streams = [(mod_lat[l],) + tuple(_out_proj(hgo_l, mla_l, x_lat, mod_lat[l], wo, g2, rt))]
            if ctx_out:
                mla_c = _attention(q_c, [k_c], [v_c])
                streams.append((mod_ctx[l],) + tuple(_out_proj(hgo_c, mla_c, x_ctx, mod_ctx[l], wo, g2, rt)))
        else:
            pw = pool_w[j].astype(BF16)
            pool_args = (row(norm1_g[l]), pw, row(pool_scale[j]), g2, rt)
            streams = [(mod_lat[l],) + tuple(_pool(x_lat, mod_lat[l], *pool_args))]
            if ctx_out:
                streams.append((mod_ctx[l],) + tuple(_pool(x_ctx, mod_ctx[l], *pool_args)))
        outs = _moe([(xn, mm, h2, aff) for mm, xn, h2, aff in streams], exp_wg, exp_wu, exp_wd, l,
                    final_g=row(final_g) if l == depth - 1 else None)
        x_lat = outs[0]
        if ctx_out:
            x_ctx = outs[1]
    return x_lat
```

```python
import functools

import numpy as np
import jax
import jax.numpy as jnp
from jax import lax
from jax.experimental import pallas as pl
from jax.experimental.pallas import tpu as pltpu

F32 = jnp.float32
BF16 = jnp.bfloat16
I32 = jnp.int32

EPS = 1e-6
GRID_W = 64
ROPE_THETA = 10000.0
HG_HEADS = 4
HG_D = 128
MLA_HEADS = 4
MLA_Q_RANK = 256
MLA_KV_RANK = 128
MLA_NOPE = 128
MLA_ROPE = 64
MLA_V = 128
MLA_SCALE = (MLA_NOPE + MLA_ROPE) ** -0.5
POOL_WINDOWS = (2, 4, 8, 16)
EC_FACTOR = 2
HG_W = HG_HEADS * HG_D

LANES = 128
CHUNK = 64
HG_HEADS_PER_STEP = 1
HG_CHAINS = 16
POOL_HALO = 8
SUBLANES = 8
BF16_ROWS = 16
GATHER_WIN = LANES + BF16_ROWS
GATHER_FAST_WIN = 48
SCATTER_WIN = 256
SCATTER_FAST_WIN = 64
ATTN_Q_SUB = 512
ATTN_Q_TILE = 1024
LOG2E = 1.4426950408889634
ROW_TILE = 512
FF_TILE = 512
ADA_COL_TILES = 4
V7X_VMEM_BYTES = 64 * 1024 * 1024
VMEM_LIMIT_V7X = V7X_VMEM_BYTES - 8 * 1024 * 1024


def _cparams(*sem):
    return pltpu.CompilerParams(dimension_semantics=sem, vmem_limit_bytes=VMEM_LIMIT_V7X)


def _nt(a, b):
    return lax.dot_general(a, b, (((1,), (1,)), ((), ())), preferred_element_type=F32)


def _tn(a, b):
    return lax.dot_general(a, b, (((0,), (0,)), ((), ())), preferred_element_type=F32)


def _dot(a, b):
    return jnp.dot(a, b, preferred_element_type=F32)


def _sigmoid(x):
    return 1.0 / (1.0 + jnp.exp(-x))


def _rms(x, g):
    return x * lax.rsqrt(jnp.mean(x * x, axis=-1, keepdims=True) + EPS) * g


def _modulate(x, g, shift, scale):
    return _rms(x, g) * (1.0 + scale) + shift


def _ada_kernel(c_ref, w_ref, b_ref, o_ref):
    c = c_ref[...]
    s = (c * _sigmoid(c)).astype(BF16)
    o_ref[0] = _dot(s, w_ref[0].astype(BF16)) + b_ref[0]


def _ada(cc, ada_w, ada_b):
    depth, d, d6 = ada_w.shape
    tn = d6 // ADA_COL_TILES
    rows = cc.shape[0]
    return pl.pallas_call(
        _ada_kernel,
        grid=(depth, ADA_COL_TILES),
        in_specs=[pl.BlockSpec((rows, d), lambda l, j: (0, 0)),
                  pl.BlockSpec((1, d, tn), lambda l, j: (l, 0, j)),
                  pl.BlockSpec((1, 1, tn), lambda l, j: (l, 0, j))],
        out_specs=pl.BlockSpec((1, rows, tn), lambda l, j: (l, 0, j)),
        out_shape=jax.ShapeDtypeStruct((depth, rows, d6), F32),
        name="ada_mod",
        compiler_params=_cparams("arbitrary", "arbitrary"),
    )(cc, ada_w, ada_b.reshape(depth, 1, d6))


def _in_kernel(x_ref, mod_ref, g1_ref, win_ref, qng_ref, wq_ref, kvg_ref, wkv_ref, cs1_ref, cs2_ref,
               hg_ref, q_ref, k_ref, vt_ref):
    m = mod_ref[0]
    h = _modulate(x_ref[0], g1_ref[...], m[0:1], m[1:2]).astype(BF16)
    proj = _dot(h, win_ref[...])
    o = 5 * HG_W
    hg_ref[0] = proj[:, :o]
    qa = proj[:, o:o + MLA_Q_RANK]
    kva = proj[:, o + MLA_Q_RANK:o + MLA_Q_RANK + MLA_KV_RANK]
    kp = proj[:, o + MLA_Q_RANK + MLA_KV_RANK:]
    qup = _dot(_rms(qa, qng_ref[...]).astype(BF16), wq_ref[...])
    kvup = _dot(_rms(kva, kvg_ref[...]).astype(BF16), wkv_ref[...])
    cs1 = cs1_ref[...]
    cs2 = cs2_ref[...]
    half = LANES // 2
    kpe = (kp * cs1 + pltpu.roll(kp, half, axis=1) * cs2).astype(BF16)
    nn = MLA_HEADS * MLA_NOPE
    qs = MLA_SCALE * LOG2E
    for hd in range(MLA_HEADS):
        p = qup[:, nn + LANES * hd:nn + LANES * (hd + 1)]
        qpe = p * cs1 + pltpu.roll(p, half, axis=1) * cs2
        q_ref[0, :, 2 * LANES * hd:2 * LANES * hd + LANES] = (qup[:, LANES * hd:LANES * (hd + 1)] * qs).astype(BF16)
        q_ref[0, :, 2 * LANES * hd + LANES:2 * LANES * (hd + 1)] = (qpe * qs).astype(BF16)
        k_ref[0, :, 2 * LANES * hd:2 * LANES * hd + LANES] = kvup[:, LANES * hd:LANES * (hd + 1)].astype(BF16)
        k_ref[0, :, 2 * LANES * hd + LANES:2 * LANES * (hd + 1)] = kpe
    vt_ref[0] = kvup[:, nn:].T.astype(BF16)


def _in_proj(x, mod, g1, win, qng, wq, kvg, wkv, cs1, cs2):
    b, n, d = x.shape
    tm = min(ROW_TILE, n)
    wcols = win.shape[1]
    full = lambda shape: pl.BlockSpec(shape, lambda i, j: (0,) * len(shape))
    return pl.pallas_call(
        _in_kernel,
        grid=(b, n // tm),
        in_specs=[pl.BlockSpec((1, tm, d), lambda i, j: (i, j, 0)),
                  pl.BlockSpec((1, 6, d), lambda i, j: (i, 0, 0)),
                  full((1, d)), full((d, wcols)), full((1, MLA_Q_RANK)), full(wq.shape),
                  full((1, MLA_KV_RANK)), full(wkv.shape),
                  pl.BlockSpec((tm, LANES), lambda i, j: (j, 0)),
                  pl.BlockSpec((tm, LANES), lambda i, j: (j, 0))],
        out_specs=[pl.BlockSpec((1, tm, 5 * HG_W), lambda i, j: (i, j, 0)),
                   pl.BlockSpec((1, tm, 2 * LANES * MLA_HEADS), lambda i, j: (i, j, 0)),
                   pl.BlockSpec((1, tm, 2 * LANES * MLA_HEADS), lambda i, j: (i, j, 0)),
                   pl.BlockSpec((1, MLA_HEADS * MLA_V, tm), lambda i, j: (i, 0, j))],
        out_shape=[jax.ShapeDtypeStruct((b, n, 5 * HG_W), F32),
                   jax.ShapeDtypeStruct((b, n, 2 * LANES * MLA_HEADS), BF16),
                   jax.ShapeDtypeStruct((b, n, 2 * LANES * MLA_HEADS), BF16),
                   jax.ShapeDtypeStruct((b, MLA_HEADS * MLA_V, n), BF16)],
        name="in_proj",
        compiler_params=_cparams("arbitrary", "arbitrary"),
    )(x, mod, g1, win, qng, wq, kvg, wkv, cs1, cs2)


def _attn_kernel(*refs, nparts):
    q_ref = refs[0]
    k_refs = refs[1:1 + nparts]
    v_refs = refs[1 + nparts:1 + 2 * nparts]
    o_ref = refs[1 + 2 * nparts]
    tq = q_ref.shape[1]
    sub = min(ATTN_Q_SUB, tq)
    qk = lambda q0: [_nt(k[0], q_ref[0, q0:q0 + sub, :]) for k in k_refs]
    nxt = qk(0)
    for q0 in range(0, tq, sub):
        s = nxt
        if q0 + sub < tq:
            nxt = qk(q0 + sub)
        m = functools.reduce(jnp.maximum, [jnp.max(x, axis=0, keepdims=True) for x in s])
        p = [jnp.exp2(x - m) for x in s]
        den = functools.reduce(jnp.add, [jnp.sum(x, axis=0, keepdims=True) for x in p])
        acc = functools.reduce(jnp.add, [_dot(vt[0], x.astype(BF16)) for vt, x in zip(v_refs, p)])
        o_ref[0, q0:q0 + sub, :] = (acc / den).T.astype(o_ref.dtype)


def _attention(q, ks, vts):
    b, n, _ = q.shape
    tq = min(ATTN_Q_TILE, n)
    nparts = len(ks)
    in_specs = [pl.BlockSpec((1, tq, 2 * LANES), lambda i, h, j: (i, j, h))]
    in_specs += [pl.BlockSpec((1, k.shape[1], 2 * LANES), lambda i, h, j: (i, 0, h)) for k in ks]
    in_specs += [pl.BlockSpec((1, MLA_V, v.shape[2]), lambda i, h, j: (i, h, 0)) for v in vts]
    return pl.pallas_call(
        functools.partial(_attn_kernel, nparts=nparts),
        grid=(b, MLA_HEADS, n // tq),
        in_specs=in_specs,
        out_specs=pl.BlockSpec((1, tq, MLA_V), lambda i, h, j: (i, j, h)),
        out_shape=jax.ShapeDtypeStruct((b, n, MLA_HEADS * MLA_V), BF16),
        name="mla_attention",
        compiler_params=_cparams("arbitrary", "arbitrary", "arbitrary"),
    )(q, *ks, *vts)


_HG_HALVES = (1, 2, 4, 8, 16, 32)


def _hgrn_masks():
    t = np.arange(CHUNK)[:, None]
    s = np.arange(CHUNK)[None, :]
    masks_f = [(t == s)]
    for h in _HG_HALVES:
        masks_f.append((t // (2 * h) == s // (2 * h)) & (t % (2 * h) >= h) & (s % (2 * h) < h))
    masks_f = np.stack(masks_f).astype(np.float32)
    return masks_f, np.transpose(masks_f, (0, 2, 1))


def _hg_block_decays(f, fw):
    c = CHUNK
    pre, suf = [f], [jnp.ones_like(f)]
    h = 1
    while h < c:
        blk = 2 * h
        src_early = h - 1 if fw else h
        src_late = blk - 1 if fw else 0
        p = pre[-1]
        if blk <= SUBLANES:
            p3 = p.reshape(c // SUBLANES, SUBLANES, HG_D)
            r = lax.broadcasted_iota(I32, p3.shape, 1)
            tot_e = jnp.ones_like(p3)
            tot_l = jnp.ones_like(p3)
            for kb in range(SUBLANES // blk):
                inblk = jnp.right_shift(r, blk.bit_length() - 1) == kb
                tot_e = jnp.where(inblk, p3[:, kb * blk + src_early:kb * blk + src_early + 1, :], tot_e)
                tot_l = jnp.where(inblk, p3[:, kb * blk + src_late:kb * blk + src_late + 1, :], tot_l)
        else:
            p3 = p.reshape(c // blk, blk, HG_D)
            r = lax.broadcasted_iota(I32, p3.shape, 1)
            tot_e = jnp.broadcast_to(p3[:, src_early:src_early + 1, :], p3.shape)
            tot_l = jnp.broadcast_to(p3[:, src_late:src_late + 1, :], p3.shape)
        in_first = jnp.bitwise_and(r, blk - 1) < h
        late = jnp.logical_not(in_first) if fw else in_first
        s3 = suf[-1].reshape(p3.shape)
        pre.append(jnp.where(late, p3 * tot_e, p3).reshape(c, HG_D))
        suf.append(jnp.where(late, s3, s3 * tot_l).reshape(c, HG_D))
        h = blk
    return pre, suf


def _hg_chunk(q, z, v, lb, onemlb, masks, st, fw, want_out):
    t = jnp.exp(-jnp.abs(z))
    r = 1.0 / (1.0 + t)
    pos = z >= 0.0
    f = lb + onemlb * jnp.where(pos, r, t * r)
    k = onemlb * jnp.where(pos, t * r, r)
    pre, suf = _hg_block_decays(f, fw)
    vb = v.astype(BF16)
    whole = pre[-1]
    kdl = (k * suf[-1]).astype(BF16)
    last = CHUNK - 1 if fw else 0
    dec = whole[last:last + 1]
    out = None
    if want_out:
        kb = k.astype(BF16)
        att = masks[0] * _nt(q.astype(BF16), kb)
        for i in range(len(_HG_HALVES)):
            a = (q * pre[i]).astype(BF16)
            bm = kb if i == 0 else (k * suf[i]).astype(BF16)
            att = att + masks[1 + i] * _nt(a, bm)
        out = _dot(att.astype(BF16), vb) + _nt((q * whole).astype(BF16), st.astype(BF16))
    st_new = st * dec + _tn(vb, kdl)
    return out, st_new


def _hgrn_kernel(*refs, has_ctx, ctx_out):
    it = iter(refs)
    q_ref, zf_ref, zb_ref, v_ref, g_ref = (next(it) for _ in range(5))
    if has_ctx:
        qc_ref, zfc_ref, zbc_ref, vc_ref, gc_ref = (next(it) for _ in range(5))
    lb_ref, gn_ref, mkf_ref, mkb_ref = (next(it) for _ in range(4))
    o_ref = next(it)
    oc_ref = next(it) if ctx_out else None
    of_s, ob_s = next(it), next(it)
    if ctx_out:
        ocf_s, ocb_s = next(it), next(it)

    lbp = lb_ref[...]
    masks = (mkf_ref[...], mkb_ref[...])
    zrefs = (zf_ref, zb_ref)

    def seq_pass(q_r, z_rs, v_r, o_ss, n, states, want_out):
        nc = n // CHUNK

        def body(ci, carry):
            new = []
            for hd in range(HG_HEADS_PER_STEP):
                cols = slice(hd * HG_D, (hd + 1) * HG_D)
                for d in range(2):
                    cidx = ci if d == 0 else nc - 1 - ci
                    off = pl.multiple_of(cidx * CHUNK, CHUNK)
                    rows = pl.ds(off, CHUNK)
                    qv = q_r[0, rows, cols] if want_out else None
                    out, st = _hg_chunk(qv, z_rs[d][0, rows, cols], v_r[0, rows, cols],
                                        lbp[2 * d:2 * d + 1, cols], lbp[2 * d + 1:2 * d + 2, cols],
                                        masks[d], carry[2 * hd + d], d == 0, want_out)
                    if want_out:
                        o_ss[d][rows, cols] = out
                    new.append(st)
            return tuple(new)

        return lax.fori_loop(0, nc, body, states, unroll=HG_CHAINS // (2 * HG_HEADS_PER_STEP))

    def readout(o_ss, g_r, out_r, n):
        tr = min(ROW_TILE, n)

        def body(i, _):
            rows = pl.ds(pl.multiple_of(i * tr, tr), tr)
            for hd in range(HG_HEADS_PER_STEP):
                cols = slice(hd * HG_D, (hd + 1) * HG_D)
                o = o_ss[0][rows, cols] + o_ss[1][rows, cols]
                g = g_r[0, rows, cols]
                out_r[0, rows, cols] = (_rms(o, gn_ref[...]) * (g * _sigmoid(g))).astype(out_r.dtype)
            return 0

        lax.fori_loop(0, n // tr, body, 0)

    zero = jnp.zeros((HG_D, HG_D), F32)
    states = (zero,) * (2 * HG_HEADS_PER_STEP)
    if has_ctx:
        nctx = vc_ref.shape[1]
        states = seq_pass(qc_ref, (zfc_ref, zbc_ref), vc_ref, (ocf_s, ocb_s) if ctx_out else None,
                          nctx, states, ctx_out)
        if ctx_out:
            readout((ocf_s, ocb_s), gc_ref, oc_ref, nctx)
    n = v_ref.shape[1]
    seq_pass(q_ref, zrefs, v_ref, (of_s, ob_s), n, states, True)
    readout((of_s, ob_s), g_ref, o_ref, n)


def _hgrn(hg, hg_ctx, lbp, gn, ctx_out):
    b, n, _ = hg.shape
    has_ctx = hg_ctx is not None
    consts = [jnp.asarray(a, F32) for a in _hgrn_masks()]

    hw = HG_HEADS_PER_STEP * HG_D
    groups = HG_HEADS // HG_HEADS_PER_STEP

    def part(nn, k):
        return pl.BlockSpec((1, nn, hw), lambda i, h, k=k: (i, 0, k * groups + h))

    in_specs = [part(n, k) for k in range(5)]
    args = [hg] * 5
    if has_ctx:
        nctx = hg_ctx.shape[1]
        in_specs += [part(nctx, k) for k in range(5)]
        args += [hg_ctx] * 5
    in_specs += [pl.BlockSpec((4, hw), lambda i, h: (0, h)),
                 pl.BlockSpec((1, HG_D), lambda i, h: (0, 0))]
    in_specs += [pl.BlockSpec(a.shape, lambda i, h, nd=a.ndim: (0,) * nd) for a in consts]
    args += [lbp, gn] + consts
    out_specs = [pl.BlockSpec((1, n, hw), lambda i, h: (i, 0, h))]
    out_shape = [jax.ShapeDtypeStruct((b, n, HG_W), BF16)]
    scratch = [pltpu.VMEM((n, hw), F32), pltpu.VMEM((n, hw), F32)]
    if ctx_out:
        out_specs.append(pl.BlockSpec((1, nctx, hw), lambda i, h: (i, 0, h)))
        out_shape.append(jax.ShapeDtypeStruct((b, nctx, HG_W), BF16))
        scratch += [pltpu.VMEM((nctx, hw), F32), pltpu.VMEM((nctx, hw), F32)]
    res = pl.pallas_call(
        functools.partial(_hgrn_kernel, has_ctx=has_ctx, ctx_out=ctx_out),
        grid=(b, groups),
        in_specs=in_specs, out_specs=out_specs, out_shape=out_shape, scratch_shapes=scratch,
        name="hgrn_scan",
        compiler_params=_cparams("arbitrary", "arbitrary"),
    )(*args)
    return (res[0], res[1]) if ctx_out else (res[0], None)


def _route_tile(x, m, g2, rt):
    h = _modulate(x, g2, m[3:4], m[4:5])
    hb = h.astype(BF16)
    hlo = (h - hb.astype(F32)).astype(BF16)
    rhi = rt.astype(BF16)
    rlo = (rt - rhi.astype(F32)).astype(BF16)
    logits = _nt(rhi, hb) + (_nt(rhi, hlo) + _nt(rlo, hb))
    ex = jnp.exp(logits - jnp.max(logits, axis=0, keepdims=True))
    return hb, ex / jnp.sum(ex, axis=0, keepdims=True)


def _mixer_out_specs(b, n, d, e, tm):
    specs = [pl.BlockSpec((1, tm, d), lambda i, j: (i, j, 0)),
             pl.BlockSpec((1, tm, d), lambda i, j: (i, j, 0)),
             pl.BlockSpec((1, e, tm), lambda i, j: (i, 0, j))]
    shapes = [jax.ShapeDtypeStruct((b, n, d), F32),
              jax.ShapeDtypeStruct((b, n, d), BF16),
              jax.ShapeDtypeStruct((b, e, n), F32)]
    return specs, shapes


def _out_kernel(hg_ref, mla_ref, x_ref, mod_ref, w_ref, g2_ref, rt_ref, o_ref, h_ref, aff_ref):
    w = w_ref[...]
    m = mod_ref[0]
    y = _dot(hg_ref[0], w[:HG_W]) + _dot(mla_ref[0], w[HG_W:])
    xn = x_ref[0] + m[2:3] * y
    o_ref[0] = xn
    h_ref[0], aff_ref[0] = _route_tile(xn, m, g2_ref[...], rt_ref[...])


def _out_proj(hgo, mla, x, mod, w, g2, rt):
    b, n, d = x.shape
    e = rt.shape[0]
    tm = min(ROW_TILE, n)
    out_specs, out_shape = _mixer_out_specs(b, n, d, e, tm)
    return pl.pallas_call(
        _out_kernel,
        grid=(b, n // tm),
        in_specs=[pl.BlockSpec((1, tm, HG_W), lambda i, j: (i, j, 0)),
                  pl.BlockSpec((1, tm, MLA_HEADS * MLA_V), lambda i, j: (i, j, 0)),
                  pl.BlockSpec((1, tm, d), lambda i, j: (i, j, 0)),
                  pl.BlockSpec((1, 6, d), lambda i, j: (i, 0, 0)),
                  pl.BlockSpec(w.shape, lambda i, j: (0, 0)),
                  pl.BlockSpec((1, d), lambda i, j: (0, 0)),
                  pl.BlockSpec((e, d), lambda i, j: (0, 0))],
        out_specs=out_specs, out_shape=out_shape,
        name="out_proj",
        compiler_params=_cparams("arbitrary", "arbitrary"),
    )(hgo, mla, x, mod, w, g2, rt)


def _pool_kernel(x_ref, prev_ref, next_ref, mod_ref, g1_ref, pw_ref, ps_ref, g2_ref, rt_ref,
                 o_ref, h_ref, aff_ref, hs_ref, *, n):
    tm = x_ref.shape[1]
    d = x_ref.shape[2]
    gw = d // len(POOL_WINDOWS)
    m = mod_ref[0]
    base = pl.program_id(1) * tm
    xe = jnp.concatenate([prev_ref[0], x_ref[0], next_ref[0]], axis=0)
    he = _modulate(xe, g1_ref[...], m[0:1], m[1:2])
    gidx = base - POOL_HALO + lax.broadcasted_iota(I32, (tm + 2 * POOL_HALO, 1), 0)
    hs_ref[...] = jnp.where((gidx >= 0) & (gidx < n), he, 0.0)
    t = base + lax.broadcasted_iota(I32, (tm, 1), 0)
    ps = ps_ref[...]
    for gi, w in enumerate(POOL_WINDOWS):
        cols = slice(gi * gw, (gi + 1) * gw)
        acc = hs_ref[pl.ds(POOL_HALO - w // 2, tm), cols]
        for j in range(1 - w // 2, w // 2):
            acc = acc + hs_ref[pl.ds(POOL_HALO + j, tm), cols]
        lo = jnp.clip(t - w // 2, 0, n - 1)
        hi = jnp.clip(t + w // 2 - 1, 0, n - 1)
        pooled = acc / (hi - lo + 1).astype(F32) - hs_ref[pl.ds(POOL_HALO, tm), cols]
        y = _dot(pooled.astype(BF16), pw_ref[gi]) * ps[:, cols]
        o_ref[0, :, cols] = x_ref[0, :, cols] + m[2:3, cols] * y
    h_ref[0], aff_ref[0] = _route_tile(o_ref[0], m, g2_ref[...], rt_ref[...])


def _pool(x, mod, g1, pw, ps, g2, rt):
    b, n, d = x.shape
    e = rt.shape[0]
    tm = min(ROW_TILE, n)
    nb = tm // POOL_HALO
    last = n // POOL_HALO - 1
    out_specs, out_shape = _mixer_out_specs(b, n, d, e, tm)
    return pl.pallas_call(
        functools.partial(_pool_kernel, n=n),
        grid=(b, n // tm),
        in_specs=[pl.BlockSpec((1, tm, d), lambda i, j: (i, j, 0)),
                  pl.BlockSpec((1, POOL_HALO, d), lambda i, j: (i, jnp.maximum(j * nb - 1, 0), 0)),
                  pl.BlockSpec((1, POOL_HALO, d), lambda i, j: (i, jnp.minimum((j + 1) * nb, last), 0)),
                  pl.BlockSpec((1, 6, d), lambda i, j: (i, 0, 0)),
                  pl.BlockSpec((1, d), lambda i, j: (0, 0)),
                  pl.BlockSpec(pw.shape, lambda i, j: (0, 0, 0)),
                  pl.BlockSpec((1, d), lambda i, j: (0, 0)),
                  pl.BlockSpec((1, d), lambda i, j: (0, 0)),
                  pl.BlockSpec((e, d), lambda i, j: (0, 0))],
        out_specs=out_specs, out_shape=out_shape,
        scratch_shapes=[pltpu.VMEM((tm + 2 * POOL_HALO, d), F32)],
        name="pool_mixer",
        compiler_params=_cparams("arbitrary", "arbitrary"),
    )(x, x, x, mod, g1, pw, ps, g2, rt)


def _topk_kernel(aff_ref, selpos_ref, gsel_ref, st_ref, stv_ref, most_ref, *, cap):
    aff = aff_ref[0]
    e, n = aff.shape
    nt = n // LANES
    capf = float(cap)

    def bit_step(i, tau):
        cand = tau | jnp.left_shift(jnp.int32(1), 30 - i)
        cnt = jnp.sum(jnp.where(aff >= pltpu.bitcast(cand, F32), 1.0, 0.0), axis=1, keepdims=True)
        return jnp.where(cnt >= capf, cand, tau)

    tau = lax.fori_loop(0, 31, bit_step, jnp.zeros((e, 1), I32))
    gt = aff >= pltpu.bitcast(tau + 1, F32)
    eq = (aff >= pltpu.bitcast(tau, F32)) & jnp.logical_not(gt)
    need = capf - jnp.sum(jnp.where(gt, 1.0, 0.0), axis=1, keepdims=True)

    r = lax.broadcasted_iota(I32, (LANES, LANES), 0)
    c = lax.broadcasted_iota(I32, (LANES, LANES), 1)
    upper = jnp.where(r < c, 1.0, 0.0).astype(BF16)

    def excl_prefix(mask_f):
        off = jnp.zeros((e, 1), F32)
        outs, starts = [], []
        for tb in range(nt):
            blk = mask_f[:, tb * LANES:(tb + 1) * LANES]
            outs.append(_dot(blk.astype(BF16), upper) + off)
            starts.append(off)
            off = off + jnp.sum(blk, axis=1, keepdims=True)
        return jnp.concatenate(outs, axis=1), starts

    eq_rank, _ = excl_prefix(jnp.where(eq, 1.0, 0.0))
    sel = gt | (eq & (eq_rank < need))
    pos, starts = excl_prefix(jnp.where(sel, 1.0, 0.0))
    selpos = jnp.where(sel, pos, -1.0)
    selpos_ref[0] = selpos.astype(I32)

    gsel_ref[0] = jnp.where(sel, aff, 0.0)

    lane = lax.broadcasted_iota(I32, (e, nt), 1)
    st = jnp.zeros((e, nt), F32)
    most = jnp.zeros((e, 1), F32)
    for tb in range(nt):
        st = jnp.where(lane == tb, starts[tb], st)
        stv_ref[0, tb] = starts[tb].astype(I32)
        nxt = starts[tb + 1] if tb + 1 < nt else capf
        most = jnp.maximum(most, nxt - starts[tb])
    st_ref[0] = st.astype(I32)
    most_ref[0] = jnp.max(most, axis=0, keepdims=True).astype(I32)


def _topk(aff_t, cap):
    b, e, n = aff_t.shape
    nt = n // LANES
    return pl.pallas_call(
        functools.partial(_topk_kernel, cap=cap),
        grid=(b,),
        in_specs=[pl.BlockSpec((1, e, n), lambda i: (i, 0, 0))],
        out_specs=[pl.BlockSpec((1, e, n), lambda i: (i, 0, 0)),
                   pl.BlockSpec((1, e, n), lambda i: (i, 0, 0)),
                   pl.BlockSpec((1, e, nt), lambda i: (i, 0, 0)),
                   pl.BlockSpec((1, nt, e, 1), lambda i: (i, 0, 0, 0)),
                   pl.BlockSpec((1, 1, 1), lambda i: (i, 0, 0))],
        out_shape=[jax.ShapeDtypeStruct((b, e, n), I32),
                   jax.ShapeDtypeStruct((b, e, n), F32),
                   jax.ShapeDtypeStruct((b, e, nt), I32),
                   jax.ShapeDtypeStruct((b, nt, e, 1), I32),
                   jax.ShapeDtypeStruct((b, 1, 1), I32)],
        name="moe_topk",
        compiler_params=_cparams("arbitrary"),
    )(aff_t)


def _gather_kernel(st_ref, most_ref, selpos_ref, gsel_ref, stv_ref, h_ref, xs_ref, gs_ref, *, cap):
    bi = pl.program_id(0)
    ne = xs_ref.shape[0]
    nt = h_ref.shape[1] // LANES
    first_cols = pl.program_id(1) == 0
    xs_ref[...] = jnp.zeros_like(xs_ref)

    @pl.when(first_cols)
    def _():
        gs_ref[...] = jnp.zeros_like(gs_ref)

    def run(win, unroll, with_gate):
        riota = lax.broadcasted_iota(I32, (ne, win, LANES), 1)

        def body(tb, _):
            rows = pl.ds(pl.multiple_of(tb * LANES, LANES), LANES)
            slot = selpos_ref[0, :, pl.ds(tb, 1), :]
            p0v = jnp.minimum(jnp.bitwise_and(stv_ref[0, tb], -BF16_ROWS), cap - win)
            onehot = jnp.where(slot - p0v[:, :, None] == riota, 1.0, 0.0)
            res = _dot(onehot.reshape(ne * win, LANES).astype(BF16), h_ref[0, rows, :])
            if with_gate:
                gate = jnp.sum(onehot * gsel_ref[0, :, pl.ds(tb, 1), :], axis=-1, keepdims=True)
            for ei in range(ne):
                start = st_ref[(bi * ne + ei) * nt + tb]
                p0 = pl.multiple_of(jnp.minimum(jnp.bitwise_and(start, -BF16_ROWS), cap - win), BF16_ROWS)
                xs_ref[ei, 0, pl.ds(p0, win), :] += res[ei * win:(ei + 1) * win].astype(xs_ref.dtype)
                if with_gate:
                    gs_ref[ei, 0, pl.ds(p0, win), :] += gate[ei]
            return 0

        lax.fori_loop(0, nt, body, 0, unroll=unroll)

    fast = min(GATHER_FAST_WIN, cap)
    full = min(GATHER_WIN, cap)
    small = most_ref[bi] <= fast - BF16_ROWS
    for is_small, win, unroll in ((True, fast, min(2, nt)), (False, full, 1)):
        for with_gate in (True, False):
            @pl.when((small == is_small) & (first_cols == with_gate))
            def _(win=win, unroll=unroll, with_gate=with_gate):
                run(win, unroll, with_gate)


def _gather(st, most, selpos, gsel, stv, h2, cap):
    b, n, d = h2.shape
    e = selpos.shape[1]
    nt = n // LANES
    dh = d // 2
    grid_spec = pltpu.PrefetchScalarGridSpec(
        num_scalar_prefetch=2,
        grid=(b, d // dh),
        in_specs=[pl.BlockSpec((1, e, nt, LANES), lambda i, k, p, q: (i, 0, 0, 0)),
                  pl.BlockSpec((1, e, nt, LANES), lambda i, k, p, q: (i, 0, 0, 0)),
                  pl.BlockSpec((1, nt, e, 1), lambda i, k, p, q: (i, 0, 0, 0)),
                  pl.BlockSpec((1, n, dh), lambda i, k, p, q: (i, 0, k))],
        out_specs=[pl.BlockSpec((e, 1, cap, dh), lambda i, k, p, q: (0, i, 0, k)),
                   pl.BlockSpec((e, 1, cap, 1), lambda i, k, p, q: (0, i, 0, 0))],
    )
    return pl.pallas_call(
        functools.partial(_gather_kernel, cap=cap),
        grid_spec=grid_spec,
        out_shape=[jax.ShapeDtypeStruct((e, b, cap, d), BF16),
                   jax.ShapeDtypeStruct((e, b, cap, 1), F32)],
        name="moe_gather",
        compiler_params=_cparams("arbitrary", "arbitrary"),
    )(st.reshape(-1), most.reshape(-1), selpos.reshape(b, e, nt, LANES), gsel.reshape(b, e, nt, LANES), stv, h2)


def _ffn_kernel(*refs, ngroups):
    xs = refs[:ngroups]
    gs = refs[ngroups:2 * ngroups]
    wg_ref, wu_ref, wd_ref = refs[2 * ngroups:2 * ngroups + 3]
    ys = refs[2 * ngroups + 3:3 * ngroups + 3]
    accs = refs[3 * ngroups + 3:]
    f = pl.program_id(1)

    @pl.when(f == 0)
    def _():
        for acc_ref in accs:
            acc_ref[...] = jnp.zeros_like(acc_ref)

    wg = wu = wd = None
    for x_ref, acc_ref in zip(xs, accs):
        rows = x_ref.shape[1]
        tr = min(ROW_TILE, rows)
        for r0 in range(0, rows, tr):
            x = x_ref[0, r0:r0 + tr, :]
            wg = wg_ref[0, 0].astype(BF16) if wg is None else wg
            g = _dot(x, wg)
            wu = wu_ref[0, 0].astype(BF16) if wu is None else wu
            u = _dot(x, wu)
            wd = wd_ref[0, 0].astype(BF16) if wd is None else wd
            acc_ref[r0:r0 + tr, :] += _dot((g * _sigmoid(g) * u).astype(BF16), wd)

    @pl.when(f == pl.num_programs(1) - 1)
    def _():
        for g_ref, y_ref, acc_ref in zip(gs, ys, accs):
            y_ref[0] = (acc_ref[...] * g_ref[0]).astype(y_ref.dtype)


def _ffn(xs_list, gs_list, wg, wu, wd, layer):
    _, e, d, ff = wg.shape
    tf = min(FF_TILE, ff)
    ng = len(xs_list)
    in_specs = [pl.BlockSpec((1, x.shape[1], d), lambda i, j: (i, 0, 0)) for x in xs_list]
    in_specs += [pl.BlockSpec((1, x.shape[1], 1), lambda i, j: (i, 0, 0)) for x in xs_list]
    in_specs += [pl.BlockSpec((1, 1, d, tf), lambda i, j: (layer, i, 0, j)),
                 pl.BlockSpec((1, 1, d, tf), lambda i, j: (layer, i, 0, j)),
                 pl.BlockSpec((1, 1, tf, d), lambda i, j: (layer, i, j, 0))]
    return pl.pallas_call(
        functools.partial(_ffn_kernel, ngroups=ng),
        grid=(e, ff // tf),
        in_specs=in_specs,
        out_specs=[pl.BlockSpec((1, x.shape[1], d), lambda i, j: (i, 0, 0)) for x in xs_list],
        out_shape=[jax.ShapeDtypeStruct(x.shape, BF16) for x in xs_list],
        scratch_shapes=[pltpu.VMEM((x.shape[1], d), F32) for x in xs_list],
        name="moe_ffn",
        compiler_params=_cparams("arbitrary", "arbitrary"),
    )(*xs_list, *gs_list, wg, wu, wd)


def _combine_kernel(st_ref, most_ref, selpos_ref, stv_ref, y_ref, x_ref, mod_ref, fg_ref, o_ref, ystk_ref, *,
                    cap, nt_total, final):
    bi = pl.program_id(0)
    tg = pl.program_id(2)
    ne = y_ref.shape[0]
    tm = x_ref.shape[1]
    mgate = mod_ref[0][5:6]

    def emit(rows, acc):
        res = x_ref[0, rows, :] + mgate * acc
        o_ref[0, rows, :] = _rms(res, fg_ref[...]) if final else res

    fwin = min(SCATTER_FAST_WIN, cap)
    swin = min(SCATTER_WIN, cap)

    def window_start(ei, tb, win):
        start = st_ref[(bi * ne + ei) * nt_total + tb]
        return pl.multiple_of(jnp.minimum(jnp.bitwise_and(start, -BF16_ROWS), cap - win), BF16_ROWS)

    def fast_body(t, _):
        tb = tg * (tm // LANES) + t
        rows = pl.ds(pl.multiple_of(t * LANES, LANES), LANES)
        for ei in range(ne):
            p0 = window_start(ei, tb, fwin)
            ystk_ref[ei * fwin:(ei + 1) * fwin, :] = y_ref[ei, 0, pl.ds(p0, fwin), :]
        slot = selpos_ref[0, :, pl.ds(tb, 1), :]
        p0v = jnp.minimum(jnp.bitwise_and(stv_ref[0, tb], -BF16_ROWS), cap - fwin)
        riota = lax.broadcasted_iota(I32, (ne, fwin, LANES), 1)
        onehot_t = jnp.where(slot - p0v[:, :, None] == riota, 1.0, 0.0)
        acc = _tn(onehot_t.reshape(ne * fwin, LANES).astype(BF16), ystk_ref[...])
        emit(rows, acc)
        return 0

    def full_body(t, _):
        tb = tg * (tm // LANES) + t
        rows = pl.ds(pl.multiple_of(t * LANES, LANES), LANES)
        riota = lax.broadcasted_iota(I32, (swin, LANES), 0)
        acc = jnp.zeros((LANES, x_ref.shape[2]), F32)
        for ei in range(ne):
            p0 = window_start(ei, tb, swin)
            slot = selpos_ref[0, ei, pl.ds(tb, 1), :]
            onehot_t = jnp.where(slot - p0 == riota, 1.0, 0.0).astype(BF16)
            acc = acc + _tn(onehot_t, y_ref[ei, 0, pl.ds(p0, swin), :])
        emit(rows, acc)
        return 0

    most = most_ref[bi]

    @pl.when(most <= fwin - BF16_ROWS)
    def _():
        lax.fori_loop(0, tm // LANES, fast_body, 0)

    @pl.when(most > fwin - BF16_ROWS)
    def _():
        lax.fori_loop(0, tm // LANES, full_body, 0)


def _combine(st, most, selpos, stv, y, x, mod, cap, final_g=None):
    b, n, d = x.shape
    e = y.shape[0]
    tm = min(ROW_TILE, n)
    dh = d
    nt = n // LANES
    final = final_g is not None
    fg = final_g if final else jnp.ones((1, d), F32)
    grid_spec = pltpu.PrefetchScalarGridSpec(
        num_scalar_prefetch=2,
        grid=(b, d // dh, n // tm),
        in_specs=[pl.BlockSpec((1, e, nt, LANES), lambda i, k, j, p, q: (i, 0, 0, 0)),
                  pl.BlockSpec((1, nt, e, 1), lambda i, k, j, p, q: (i, 0, 0, 0)),
                  pl.BlockSpec((e, 1, cap, dh), lambda i, k, j, p, q: (0, i, 0, k)),
                  pl.BlockSpec((1, tm, dh), lambda i, k, j, p, q: (i, j, k)),
                  pl.BlockSpec((1, 6, dh), lambda i, k, j, p, q: (i, 0, k)),
                  pl.BlockSpec((1, dh), lambda i, k, j, p, q: (0, k))],
        out_specs=pl.BlockSpec((1, tm, dh), lambda i, k, j, p, q: (i, j, k)),
        scratch_shapes=[pltpu.VMEM((e * min(SCATTER_FAST_WIN, cap), dh), BF16)],
    )
    return pl.pallas_call(
        functools.partial(_combine_kernel, cap=cap, nt_total=nt, final=final),
        grid_spec=grid_spec,
        out_shape=jax.ShapeDtypeStruct((b, n, d), F32),
        name="moe_combine",
        compiler_params=_cparams("arbitrary", "arbitrary", "arbitrary"),
    )(st.reshape(-1), most.reshape(-1), selpos.reshape(b, e, nt, LANES), stv, y, x, mod, fg)


def _rot_cols(w):
    q = MLA_ROPE // 4
    return jnp.concatenate([-w[:, q:2 * q], w[:, :q], -w[:, 3 * q:], w[:, 2 * q:3 * q]], axis=1)


def _prep_even(w_in, wq_b, wkv_b):
    kpe = w_in[:, -MLA_ROPE:]
    win = jnp.concatenate([w_in, _rot_cols(kpe)], axis=1).astype(BF16)
    qh = wq_b.reshape(MLA_Q_RANK, MLA_HEADS, MLA_NOPE + MLA_ROPE)
    nope = [qh[:, h, :MLA_NOPE] for h in range(MLA_HEADS)]
    pe = [jnp.concatenate([qh[:, h, MLA_NOPE:], _rot_cols(qh[:, h, MLA_NOPE:])], axis=1) for h in range(MLA_HEADS)]
    wq = jnp.concatenate(nope + pe, axis=1).astype(BF16)
    kvh = wkv_b.reshape(MLA_KV_RANK, MLA_HEADS, MLA_NOPE + MLA_V)
    wkv = jnp.concatenate([kvh[:, h, :MLA_NOPE] for h in range(MLA_HEADS)]
                          + [kvh[:, h, MLA_NOPE:] for h in range(MLA_HEADS)], axis=1).astype(BF16)
    return win, wq, wkv


def _rope_tables(n):
    rows = n // GRID_W
    row = jnp.repeat(jnp.arange(rows, dtype=F32), GRID_W)
    col = jnp.tile(jnp.arange(GRID_W, dtype=F32), rows)
    half = MLA_ROPE // 2
    inv = 1.0 / (ROPE_THETA ** (jnp.arange(0, half, 2, dtype=F32) / half))
    ar = row[:, None] * inv[None, :]
    ac = col[:, None] * inv[None, :]
    ang = jnp.concatenate([ar, ar, ac, ac], axis=-1)
    z = jnp.zeros((n, LANES - MLA_ROPE), F32)
    return jnp.concatenate([jnp.cos(ang), z], axis=1), jnp.concatenate([jnp.sin(ang), z], axis=1)


def _moe(streams, wg, wu, wd, layer, final_g=None):
    routed = []
    for xx, mm, h2, aff_t in streams:
        b, n, d = xx.shape
        e = aff_t.shape[1]
        cap = EC_FACTOR * n // e
        selpos, gsel, st, stv, most = _topk(aff_t, cap)
        xs, gs = _gather(st, most, selpos, gsel, stv, h2, cap)
        routed.append((st, most, selpos, stv, cap, xs, gs))
    flat = lambda a: a.reshape(a.shape[0], -1, a.shape[3])
    ys = _ffn([flat(r[5]) for r in routed], [flat(r[6]) for r in routed], wg, wu, wd, layer)
    outs = []
    for si, ((xx, mm, _, _), (st, most, selpos, stv, cap, xs, gs), y) in enumerate(zip(streams, routed, ys)):
        outs.append(_combine(st, most, selpos, stv, y.reshape(xs.shape), xx, mm, cap,
                             final_g if si == 0 else None))
    return outs


def kernel(x, c, ctx, c_ctx, ada_w, ada_b, norm1_g, norm2_g, w_in, hg_lb, hg_norm_g, mla_qn_g, mla_wq_b,
           mla_kvn_g, mla_wkv_b, w_out, pool_w, pool_scale, router_w, exp_wg, exp_wu, exp_wd, final_g):
    b, n, d = x.shape
    nctx = ctx.shape[1]
    depth = ada_w.shape[0]
    n_even = w_in.shape[0]
    assert b + 1 <= SUBLANES and n % LANES == 0 and nctx % LANES == 0 and n % GRID_W == 0

    cc = jnp.zeros((SUBLANES, d), F32).at[:b].set(c).at[b].set(c_ctx)
    mods = _ada(cc, ada_w, ada_b)
    mod_lat = [mods[l, :b].reshape(b, 6, d) for l in range(depth)]
    mod_ctx = [jnp.broadcast_to(mods[l, b].reshape(1, 6, d), (b, 6, d)) for l in range(depth)]

    lb_all = jnp.cumsum(jax.nn.softmax(hg_lb.astype(F32), axis=0), axis=0)
    lb_all = lb_all - lb_all[:1]
    cos1, sin1 = _rope_tables(n)
    cos_c = jnp.concatenate([jnp.ones((nctx, MLA_ROPE), F32), jnp.zeros((nctx, LANES - MLA_ROPE), F32)], axis=1)
    sin_c = jnp.zeros((nctx, LANES), F32)

    last_reader = 2 * (n_even - 1)
    row = lambda v: v.reshape(1, -1)
    x_lat, x_ctx = x, ctx
    for l in range(depth):
        j = l // 2
        ctx_in = l <= last_reader
        ctx_out = l < last_reader
        g2, rt = row(norm2_g[l]), router_w[l].T
        if l % 2 == 0:
            win, wq, wkv = _prep_even(w_in[j], mla_wq_b[j], mla_wkv_b[j])
            lb = lb_all[j]
            lbp = jnp.stack([lb[0], 1.0 - lb[0], lb[1], 1.0 - lb[1]])
            common = (row(norm1_g[l]), win, row(mla_qn_g[j]), wq, row(mla_kvn_g[j]), wkv)
            hg_l, q_l, k_l, v_l = _in_proj(x_lat, mod_lat[l], *common, cos1, sin1)
            hg_c = None
            ks, vs = [k_l], [v_l]
            if ctx_in:
                hg_c, q_c, k_c, v_c = _in_proj(x_ctx, mod_ctx[l], *common, cos_c, sin_c)
                ks, vs = [k_c, k_l], [v_c, v_l]
            hgo_l, hgo_c = _hgrn(hg_l, hg_c, lbp, row(hg_norm_g[j]), ctx_out)
            mla_l = _attention(q_l, ks, vs)
            wo = w_out[j].astype(BF16)
            streams = [(mod_lat[l],) + tuple(_out_proj(hgo_l, mla_l, x_lat, mod_lat[l], wo, g2, rt))]
            if ctx_out:
                mla_c = _attention(q_c, [k_c], [v_c])
                streams.append((mod_ctx[l],) + tuple(_out_proj(hgo_c, mla_c, x_ctx, mod_ctx[l], wo, g2, rt)))
        else:
            pw = pool_w[j].astype(BF16)
            pool_args = (row(norm1_g[l]), pw, row(pool_scale[j]), g2, rt)
            streams = [(mod_lat[l],) + tuple(_pool(x_lat, mod_lat[l], *pool_args))]
            if ctx_out:
                streams.append((mod_ctx[l],) + tuple(_pool(x_ctx, mod_ctx[l], *pool_args)))
        outs = _moe([(xn, mm, h2, aff) for mm, xn, h2, aff in streams], exp_wg, exp_wu, exp_wd, l,
                    final_g=row(final_g) if l == depth - 1 else None)
        x_lat = outs[0]
        if ctx_out:
            x_ctx = outs[1]
    return x_lat
```

```python
import functools

import numpy as np
import jax
import jax.numpy as jnp
from jax import lax
from jax.experimental import pallas as pl
from jax.experimental.pallas import tpu as pltpu

F32 = jnp.float32
BF16 = jnp.bfloat16
I32 = jnp.int32

EPS = 1e-6
GRID_W = 64
ROPE_THETA = 10000.0
HG_HEADS = 4
HG_D = 128
MLA_HEADS = 4
MLA_Q_RANK = 256
MLA_KV_RANK = 128
MLA_NOPE = 128
MLA_ROPE = 64
MLA_V = 128
MLA_SCALE = (MLA_NOPE + MLA_ROPE) ** -0.5
POOL_WINDOWS = (2, 4, 8, 16)
EC_FACTOR = 2
HG_W = HG_HEADS * HG_D

LANES = 128
CHUNK = 64
HG_HEADS_PER_STEP = 1
HG_CHAINS = 16
POOL_HALO = 8
SUBLANES = 8
BF16_ROWS = 16
GATHER_WIN = LANES + BF16_ROWS
GATHER_FAST_WIN = 48
SCATTER_WIN = 256
SCATTER_FAST_WIN = 64
ATTN_Q_SUB = 512
ATTN_Q_TILE = 1024
LOG2E = 1.4426950408889634
ROW_TILE = 512
FF_TILE = 512
ADA_COL_TILES = 4
V7X_VMEM_BYTES = 64 * 1024 * 1024
VMEM_LIMIT_V7X = V7X_VMEM_BYTES - 8 * 1024 * 1024


def _cparams(*sem):
    return pltpu.CompilerParams(dimension_semantics=sem, vmem_limit_bytes=VMEM_LIMIT_V7X)


def _nt(a, b):
    return lax.dot_general(a, b, (((1,), (1,)), ((), ())), preferred_element_type=F32)


def _tn(a, b):
    return lax.dot_general(a, b, (((0,), (0,)), ((), ())), preferred_element_type=F32)


def _dot(a, b):
    return jnp.dot(a, b, preferred_element_type=F32)


def _sigmoid(x):
    return 1.0 / (1.0 + jnp.exp(-x))


def _rms(x, g):
    return x * lax.rsqrt(jnp.mean(x * x, axis=-1, keepdims=True) + EPS) * g


def _modulate(x, g, shift, scale):
    return _rms(x, g) * (1.0 + scale) + shift


def _ada_kernel(c_ref, w_ref, b_ref, o_ref):
    c = c_ref[...]
    s = (c * _sigmoid(c)).astype(BF16)
    o_ref[0] = _dot(s, w_ref[0].astype(BF16)) + b_ref[0]


def _ada(cc, ada_w, ada_b):
    depth, d, d6 = ada_w.shape
    tn = d6 // ADA_COL_TILES
    rows = cc.shape[0]
    return pl.pallas_call(
        _ada_kernel,
        grid=(depth, ADA_COL_TILES),
        in_specs=[pl.BlockSpec((rows, d), lambda l, j: (0, 0)),
                  pl.BlockSpec((1, d, tn), lambda l, j: (l, 0, j)),
                  pl.BlockSpec((1, 1, tn), lambda l, j: (l, 0, j))],
        out_specs=pl.BlockSpec((1, rows, tn), lambda l, j: (l, 0, j)),
        out_shape=jax.ShapeDtypeStruct((depth, rows, d6), F32),
        name="ada_mod",
        compiler_params=_cparams("arbitrary", "arbitrary"),
    )(cc, ada_w, ada_b.reshape(depth, 1, d6))


def _in_kernel(x_ref, mod_ref, g1_ref, win_ref, qng_ref, wq_ref, kvg_ref, wkv_ref, cs1_ref, cs2_ref,
               hg_ref, q_ref, k_ref, vt_ref):
    m = mod_ref[0]
    h = _modulate(x_ref[0], g1_ref[...], m[0:1], m[1:2]).astype(BF16)
    proj = _dot(h, win_ref[...])
    o = 5 * HG_W
    hg_ref[0] = proj[:, :o]
    qa = proj[:, o:o + MLA_Q_RANK]
    kva = proj[:, o + MLA_Q_RANK:o + MLA_Q_RANK + MLA_KV_RANK]
    kp = proj[:, o + MLA_Q_RANK + MLA_KV_RANK:]
    qup = _dot(_rms(qa, qng_ref[...]).astype(BF16), wq_ref[...])
    kvup = _dot(_rms(kva, kvg_ref[...]).astype(BF16), wkv_ref[...])
    cs1 = cs1_ref[...]
    cs2 = cs2_ref[...]
    half = LANES // 2
    kpe = (kp * cs1 + pltpu.roll(kp, half, axis=1) * cs2).astype(BF16)
    nn = MLA_HEADS * MLA_NOPE
    qs = MLA_SCALE * LOG2E
    for hd in range(MLA_HEADS):
        p = qup[:, nn + LANES * hd:nn + LANES * (hd + 1)]
        qpe = p * cs1 + pltpu.roll(p, half, axis=1) * cs2
        q_ref[0, :, 2 * LANES * hd:2 * LANES * hd + LANES] = (qup[:, LANES * hd:LANES * (hd + 1)] * qs).astype(BF16)
        q_ref[0, :, 2 * LANES * hd + LANES:2 * LANES * (hd + 1)] = (qpe * qs).astype(BF16)
        k_ref[0, :, 2 * LANES * hd:2 * LANES * hd + LANES] = kvup[:, LANES * hd:LANES * (hd + 1)].astype(BF16)
        k_ref[0, :, 2 * LANES * hd + LANES:2 * LANES * (hd + 1)] = kpe
    vt_ref[0] = kvup[:, nn:].T.astype(BF16)


def _in_proj(x, mod, g1, win, qng, wq, kvg, wkv, cs1, cs2):
    b, n, d = x.shape
    tm = min(ROW_TILE, n)
    wcols = win.shape[1]
    full = lambda shape: pl.BlockSpec(shape, lambda i, j: (0,) * len(shape))
    return pl.pallas_call(
        _in_kernel,
        grid=(b, n // tm),
        in_specs=[pl.BlockSpec((1, tm, d), lambda i, j: (i, j, 0)),
                  pl.BlockSpec((1, 6, d), lambda i, j: (i, 0, 0)),
                  full((1, d)), full((d, wcols)), full((1, MLA_Q_RANK)), full(wq.shape),
                  full((1, MLA_KV_RANK)), full(wkv.shape),
                  pl.BlockSpec((tm, LANES), lambda i, j: (j, 0)),
                  pl.BlockSpec((tm, LANES), lambda i, j: (j, 0))],
        out_specs=[pl.BlockSpec((1, tm, 5 * HG_W), lambda i, j: (i, j, 0)),
                   pl.BlockSpec((1, tm, 2 * LANES * MLA_HEADS), lambda i, j: (i, j, 0)),
                   pl.BlockSpec((1, tm, 2 * LANES * MLA_HEADS), lambda i, j: (i, j, 0)),
                   pl.BlockSpec((1, MLA_HEADS * MLA_V, tm), lambda i, j: (i, 0, j))],
        out_shape=[jax.ShapeDtypeStruct((b, n, 5 * HG_W), F32),
                   jax.ShapeDtypeStruct((b, n, 2 * LANES * MLA_HEADS), BF16),
                   jax.ShapeDtypeStruct((b, n, 2 * LANES * MLA_HEADS), BF16),
                   jax.ShapeDtypeStruct((b, MLA_HEADS * MLA_V, n), BF16)],
        name="in_proj",
        compiler_params=_cparams("arbitrary", "arbitrary"),
    )(x, mod, g1, win, qng, wq, kvg, wkv, cs1, cs2)


def _attn_kernel(*refs, nparts):
    q_ref = refs[0]
    k_refs = refs[1:1 + nparts]
    v_refs = refs[1 + nparts:1 + 2 * nparts]
    o_ref = refs[1 + 2 * nparts]
    tq = q_ref.shape[1]
    sub = min(ATTN_Q_SUB, tq)
    qk = lambda q0: [_nt(k[0], q_ref[0, q0:q0 + sub, :]) for k in k_refs]
    nxt = qk(0)
    for q0 in range(0, tq, sub):
        s = nxt
        if q0 + sub < tq:
            nxt = qk(q0 + sub)
        m = functools.reduce(jnp.maximum, [jnp.max(x, axis=0, keepdims=True) for x in s])
        p = [jnp.exp2(x - m) for x in s]
        den = functools.reduce(jnp.add, [jnp.sum(x, axis=0, keepdims=True) for x in p])
        acc = functools.reduce(jnp.add, [_dot(vt[0], x.astype(BF16)) for vt, x in zip(v_refs, p)])
        o_ref[0, q0:q0 + sub, :] = (acc / den).T.astype(o_ref.dtype)


def _attention(q, ks, vts):
    b, n, _ = q.shape
    tq = min(ATTN_Q_TILE, n)
    nparts = len(ks)
    in_specs = [pl.BlockSpec((1, tq, 2 * LANES), lambda i, h, j: (i, j, h))]
    in_specs += [pl.BlockSpec((1, k.shape[1], 2 * LANES), lambda i, h, j: (i, 0, h)) for k in ks]
    in_specs += [pl.BlockSpec((1, MLA_V, v.shape[2]), lambda i, h, j: (i, h, 0)) for v in vts]
    return pl.pallas_call(
        functools.partial(_attn_kernel, nparts=nparts),
        grid=(b, MLA_HEADS, n // tq),
        in_specs=in_specs,
        out_specs=pl.BlockSpec((1, tq, MLA_V), lambda i, h, j: (i, j, h)),
        out_shape=jax.ShapeDtypeStruct((b, n, MLA_HEADS * MLA_V), BF16),
        name="mla_attention",
        compiler_params=_cparams("arbitrary", "arbitrary", "arbitrary"),
    )(q, *ks, *vts)


_HG_HALVES = (1, 2, 4, 8, 16, 32)


def _hgrn_masks():
    t = np.arange(CHUNK)[:, None]
    s = np.arange(CHUNK)[None, :]
    masks_f = [(t == s)]
    for h in _HG_HALVES:
        masks_f.append((t // (2 * h) == s // (2 * h)) & (t % (2 * h) >= h) & (s % (2 * h) < h))
    masks_f = np.stack(masks_f).astype(np.float32)
    return masks_f, np.transpose(masks_f, (0, 2, 1))


def _hg_block_decays(f, fw):
    c = CHUNK
    pre, suf = [f], [jnp.ones_like(f)]
    h = 1
    while h < c:
        blk = 2 * h
        src_early = h - 1 if fw else h
        src_late = blk - 1 if fw else 0
        p = pre[-1]
        if blk <= SUBLANES:
            p3 = p.reshape(c // SUBLANES, SUBLANES, HG_D)
            r = lax.broadcasted_iota(I32, p3.shape, 1)
            tot_e = jnp.ones_like(p3)
            tot_l = jnp.ones_like(p3)
            for kb in range(SUBLANES // blk):
                inblk = jnp.right_shift(r, blk.bit_length() - 1) == kb
                tot_e = jnp.where(inblk, p3[:, kb * blk + src_early:kb * blk + src_early + 1, :], tot_e)
                tot_l = jnp.where(inblk, p3[:, kb * blk + src_late:kb * blk + src_late + 1, :], tot_l)
        else:
            p3 = p.reshape(c // blk, blk, HG_D)
            r = lax.broadcasted_iota(I32, p3.shape, 1)
            tot_e = jnp.broadcast_to(p3[:, src_early:src_early + 1, :], p3.shape)
            tot_l = jnp.broadcast_to(p3[:, src_late:src_late + 1, :], p3.shape)
        in_first = jnp.bitwise_and(r, blk - 1) < h
        late = jnp.logical_not(in_first) if fw else in_first
        s3 = suf[-1].reshape(p3.shape)
        pre.append(jnp.where(late, p3 * tot_e, p3).reshape(c, HG_D))
        suf.append(jnp.where(late, s3, s3 * tot_l).reshape(c, HG_D))
        h = blk
    return pre, suf


def _hg_chunk(q, z, v, lb, onemlb, masks, st, fw, want_out):
    t = jnp.exp(-jnp.abs(z))
    r = 1.0 / (1.0 + t)
    pos = z >= 0.0
    f = lb + onemlb * jnp.where(pos, r, t * r)
    k = onemlb * jnp.where(pos, t * r, r)
    pre, suf = _hg_block_decays(f, fw)
    vb = v.astype(BF16)
    whole = pre[-1]
    kdl = (k * suf[-1]).astype(BF16)
    last = CHUNK - 1 if fw else 0
    dec = whole[last:last + 1]
    out = None
    if want_out:
        kb = k.astype(BF16)
        att = masks[0] * _nt(q.astype(BF16), kb)
        for i in range(len(_HG_HALVES)):
            a = (q * pre[i]).astype(BF16)
            bm = kb if i == 0 else (k * suf[i]).astype(BF16)
            att = att + masks[1 + i] * _nt(a, bm)
        out = _dot(att.astype(BF16), vb) + _nt((q * whole).astype(BF16), st.astype(BF16))
    st_new = st * dec + _tn(vb, kdl)
    return out, st_new


def _hgrn_kernel(*refs, has_ctx, ctx_out):
    it = iter(refs)
    q_ref, zf_ref, zb_ref, v_ref, g_ref = (next(it) for _ in range(5))
    if has_ctx:
        qc_ref, zfc_ref, zbc_ref, vc_ref, gc_ref = (next(it) for _ in range(5))
    lb_ref, gn_ref, mkf_ref, mkb_ref = (next(it) for _ in range(4))
    o_ref = next(it)
    oc_ref = next(it) if ctx_out else None
    of_s, ob_s = next(it), next(it)
    if ctx_out:
        ocf_s, ocb_s = next(it), next(it)

    lbp = lb_ref[...]
    masks = (mkf_ref[...], mkb_ref[...])
    zrefs = (zf_ref, zb_ref)

    def seq_pass(q_r, z_rs, v_r, o_ss, n, states, want_out):
        nc = n // CHUNK

        def body(ci, carry):
            new = []
            for hd in range(HG_HEADS_PER_STEP):
                cols = slice(hd * HG_D, (hd + 1) * HG_D)
                for d in range(2):
                    cidx = ci if d == 0 else nc - 1 - ci
                    off = pl.multiple_of(cidx * CHUNK, CHUNK)
                    rows = pl.ds(off, CHUNK)
                    qv = q_r[0, rows, cols] if want_out else None
                    out, st = _hg_chunk(qv, z_rs[d][0, rows, cols], v_r[0, rows, cols],
                                        lbp[2 * d:2 * d + 1, cols], lbp[2 * d + 1:2 * d + 2, cols],
                                        masks[d], carry[2 * hd + d], d == 0, want_out)
                    if want_out:
                        o_ss[d][rows, cols] = out
                    new.append(st)
            return tuple(new)

        return lax.fori_loop(0, nc, body, states, unroll=HG_CHAINS // (2 * HG_HEADS_PER_STEP))

    def readout(o_ss, g_r, out_r, n):
        tr = min(ROW_TILE, n)

        def body(i, _):
            rows = pl.ds(pl.multiple_of(i * tr, tr), tr)
            for hd in range(HG_HEADS_PER_STEP):
                cols = slice(hd * HG_D, (hd + 1) * HG_D)
                o = o_ss[0][rows, cols] + o_ss[1][rows, cols]
                g = g_r[0, rows, cols]
                out_r[0, rows, cols] = (_rms(o, gn_ref[...]) * (g * _sigmoid(g))).astype(out_r.dtype)
            return 0

        lax.fori_loop(0, n // tr, body, 0)

    zero = jnp.zeros((HG_D, HG_D), F32)
    states = (zero,) * (2 * HG_HEADS_PER_STEP)
    if has_ctx:
        nctx = vc_ref.shape[1]
        states = seq_pass(qc_ref, (zfc_ref, zbc_ref), vc_ref, (ocf_s, ocb_s) if ctx_out else None,
                          nctx, states, ctx_out)
        if ctx_out:
            readout((ocf_s, ocb_s), gc_ref, oc_ref, nctx)
    n = v_ref.shape[1]
    seq_pass(q_ref, zrefs, v_ref, (of_s, ob_s), n, states, True)
    readout((of_s, ob_s), g_ref, o_ref, n)


def _hgrn(hg, hg_ctx, lbp, gn, ctx_out):
    b, n, _ = hg.shape
    has_ctx = hg_ctx is not None
    consts = [jnp.asarray(a, F32) for a in _hgrn_masks()]

    hw = HG_HEADS_PER_STEP * HG_D
    groups = HG_HEADS // HG_HEADS_PER_STEP

    def part(nn, k):
        return pl.BlockSpec((1, nn, hw), lambda i, h, k=k: (i, 0, k * groups + h))

    in_specs = [part(n, k) for k in range(5)]
    args = [hg] * 5
    if has_ctx:
        nctx = hg_ctx.shape[1]
        in_specs += [part(nctx, k) for k in range(5)]
        args += [hg_ctx] * 5
    in_specs += [pl.BlockSpec((4, hw), lambda i, h: (0, h)),
                 pl.BlockSpec((1, HG_D), lambda i, h: (0, 0))]
    in_specs += [pl.BlockSpec(a.shape, lambda i, h, nd=a.ndim: (0,) * nd) for a in consts]
    args += [lbp, gn] + consts
    out_specs = [pl.BlockSpec((1, n, hw), lambda i, h: (i, 0, h))]
    out_shape = [jax.ShapeDtypeStruct((b, n, HG_W), BF16)]
    scratch = [pltpu.VMEM((n, hw), F32), pltpu.VMEM((n, hw), F32)]
    if ctx_out:
        out_specs.append(pl.BlockSpec((1, nctx, hw), lambda i, h: (i, 0, h)))
        out_shape.append(jax.ShapeDtypeStruct((b, nctx, HG_W), BF16))
        scratch += [pltpu.VMEM((nctx, hw), F32), pltpu.VMEM((nctx, hw), F32)]
    res = pl.pallas_call(
        functools.partial(_hgrn_kernel, has_ctx=has_ctx, ctx_out=ctx_out),
        grid=(b, groups),
        in_specs=in_specs, out_specs=out_specs, out_shape=out_shape, scratch_shapes=scratch,
        name="hgrn_scan",
        compiler_params=_cparams("arbitrary", "arbitrary"),
    )(*args)
    return (res[0], res[1]) if ctx_out else (res[0], None)


def _route_tile(x, m, g2, rt):
    h = _modulate(x, g2, m[3:4], m[4:5])
    hb = h.astype(BF16)
    hlo = (h - hb.astype(F32)).astype(BF16)
    rhi = rt.astype(BF16)
    rlo = (rt - rhi.astype(F32)).astype(BF16)
    logits = _nt(rhi, hb) + (_nt(rhi, hlo) + _nt(rlo, hb))
    ex = jnp.exp(logits - jnp.max(logits, axis=0, keepdims=True))
    return hb, ex / jnp.sum(ex, axis=0, keepdims=True)


def _mixer_out_specs(b, n, d, e, tm):
    specs = [pl.BlockSpec((1, tm, d), lambda i, j: (i, j, 0)),
             pl.BlockSpec((1, tm, d), lambda i, j: (i, j, 0)),
             pl.BlockSpec((1, e, tm), lambda i, j: (i, 0, j))]
    shapes = [jax.ShapeDtypeStruct((b, n, d), F32),
              jax.ShapeDtypeStruct((b, n, d), BF16),
              jax.ShapeDtypeStruct((b, e, n), F32)]
    return specs, shapes


def _out_kernel(hg_ref, mla_ref, x_ref, mod_ref, w_ref, g2_ref, rt_ref, o_ref, h_ref, aff_ref):
    w = w_ref[...]
    m = mod_ref[0]
    y = _dot(hg_ref[0], w[:HG_W]) + _dot(mla_ref[0], w[HG_W:])
    xn = x_ref[0] + m[2:3] * y
    o_ref[0] = xn
    h_ref[0], aff_ref[0] = _route_tile(xn, m, g2_ref[...], rt_ref[...])


def _out_proj(hgo, mla, x, mod, w, g2, rt):
    b, n, d = x.shape
    e = rt.shape[0]
    tm = min(ROW_TILE, n)
    out_specs, out_shape = _mixer_out_specs(b, n, d, e, tm)
    return pl.pallas_call(
        _out_kernel,
        grid=(b, n // tm),
        in_specs=[pl.BlockSpec((1, tm, HG_W), lambda i, j: (i, j, 0)),
                  pl.BlockSpec((1, tm, MLA_HEADS * MLA_V), lambda i, j: (i, j, 0)),
                  pl.BlockSpec((1, tm, d), lambda i, j: (i, j, 0)),
                  pl.BlockSpec((1, 6, d), lambda i, j: (i, 0, 0)),
                  pl.BlockSpec(w.shape, lambda i, j: (0, 0)),
                  pl.BlockSpec((1, d), lambda i, j: (0, 0)),
                  pl.BlockSpec((e, d), lambda i, j: (0, 0))],
        out_specs=out_specs, out_shape=out_shape,
        name="out_proj",
        compiler_params=_cparams("arbitrary", "arbitrary"),
    )(hgo, mla, x, mod, w, g2, rt)


def _pool_kernel(x_ref, prev_ref, next_ref, mod_ref, g1_ref, pw_ref, ps_ref, g2_ref, rt_ref,
                 o_ref, h_ref, aff_ref, hs_ref, *, n):
    tm = x_ref.shape[1]
    d = x_ref.shape[2]
    gw = d // len(POOL_WINDOWS)
    m = mod_ref[0]
    base = pl.program_id(1) * tm
    xe = jnp.concatenate([prev_ref[0], x_ref[0], next_ref[0]], axis=0)
    he = _modulate(xe, g1_ref[...], m[0:1], m[1:2])
    gidx = base - POOL_HALO + lax.broadcasted_iota(I32, (tm + 2 * POOL_HALO, 1), 0)
    hs_ref[...] = jnp.where((gidx >= 0) & (gidx < n), he, 0.0)
    t = base + lax.broadcasted_iota(I32, (tm, 1), 0)
    ps = ps_ref[...]
    for gi, w in enumerate(POOL_WINDOWS):
        cols = slice(gi * gw, (gi + 1) * gw)
        acc = hs_ref[pl.ds(POOL_HALO - w // 2, tm), cols]
        for j in range(1 - w // 2, w // 2):
            acc = acc + hs_ref[pl.ds(POOL_HALO + j, tm), cols]
        lo = jnp.clip(t - w // 2, 0, n - 1)
        hi = jnp.clip(t + w // 2 - 1, 0, n - 1)
        pooled = acc / (hi - lo + 1).astype(F32) - hs_ref[pl.ds(POOL_HALO, tm), cols]
        y = _dot(pooled.astype(BF16), pw_ref[gi]) * ps[:, cols]
        o_ref[0, :, cols] = x_ref[0, :, cols] + m[2:3, cols] * y
    h_ref[0], aff_ref[0] = _route_tile(o_ref[0], m, g2_ref[...], rt_ref[...])


def _pool(x, mod, g1, pw, ps, g2, rt):
    b, n, d = x.shape
    e = rt.shape[0]
    tm = min(ROW_TILE, n)
    nb = tm // POOL_HALO
    last = n // POOL_HALO - 1
    out_specs, out_shape = _mixer_out_specs(b, n, d, e, tm)
    return pl.pallas_call(
        functools.partial(_pool_kernel, n=n),
        grid=(b, n // tm),
        in_specs=[pl.BlockSpec((1, tm, d), lambda i, j: (i, j, 0)),
                  pl.BlockSpec((1, POOL_HALO, d), lambda i, j: (i, jnp.maximum(j * nb - 1, 0), 0)),
                  pl.BlockSpec((1, POOL_HALO, d), lambda i, j: (i, jnp.minimum((j + 1) * nb, last), 0)),
                  pl.BlockSpec((1, 6, d), lambda i, j: (i, 0, 0)),
                  pl.BlockSpec((1, d), lambda i, j: (0, 0)),
                  pl.BlockSpec(pw.shape, lambda i, j: (0, 0, 0)),
                  pl.BlockSpec((1, d), lambda i, j: (0, 0)),
                  pl.BlockSpec((1, d), lambda i, j: (0, 0)),
                  pl.BlockSpec((e, d), lambda i, j: (0, 0))],
        out_specs=out_specs, out_shape=out_shape,
        scratch_shapes=[pltpu.VMEM((tm + 2 * POOL_HALO, d), F32)],
        name="pool_mixer",
        compiler_params=_cparams("arbitrary", "arbitrary"),
    )(x, x, x, mod, g1, pw, ps, g2, rt)


def _topk_kernel(aff_ref, selpos_ref, gsel_ref, st_ref, stv_ref, most_ref, *, cap):
    aff = aff_ref[0]
    e, n = aff.shape
    nt = n // LANES
    capf = float(cap)

    def bit_step(i, tau):
        cand = tau | jnp.left_shift(jnp.int32(1), 30 - i)
        cnt = jnp.sum(jnp.where(aff >= pltpu.bitcast(cand, F32), 1.0, 0.0), axis=1, keepdims=True)
        return jnp.where(cnt >= capf, cand, tau)

    tau = lax.fori_loop(0, 31, bit_step, jnp.zeros((e, 1), I32))
    gt = aff >= pltpu.bitcast(tau + 1, F32)
    eq = (aff >= pltpu.bitcast(tau, F32)) & jnp.logical_not(gt)
    need = capf - jnp.sum(jnp.where(gt, 1.0, 0.0), axis=1, keepdims=True)

    r = lax.broadcasted_iota(I32, (LANES, LANES), 0)
    c = lax.broadcasted_iota(I32, (LANES, LANES), 1)
    upper = jnp.where(r < c, 1.0, 0.0).astype(BF16)

    def excl_prefix(mask_f):
        off = jnp.zeros((e, 1), F32)
        outs, starts = [], []
        for tb in range(nt):
            blk = mask_f[:, tb * LANES:(tb + 1) * LANES]
            outs.append(_dot(blk.astype(BF16), upper) + off)
            starts.append(off)
            off = off + jnp.sum(blk, axis=1, keepdims=True)
        return jnp.concatenate(outs, axis=1), starts

    eq_rank, _ = excl_prefix(jnp.where(eq, 1.0, 0.0))
    sel = gt | (eq & (eq_rank < need))
    pos, starts = excl_prefix(jnp.where(sel, 1.0, 0.0))
    selpos = jnp.where(sel, pos, -1.0)
    selpos_ref[0] = selpos.astype(I32)

    gsel_ref[0] = jnp.where(sel, aff, 0.0)

    lane = lax.broadcasted_iota(I32, (e, nt), 1)
    st = jnp.zeros((e, nt), F32)
    most = jnp.zeros((e, 1), F32)
    for tb in range(nt):
        st = jnp.where(lane == tb, starts[tb], st)
        stv_ref[0, tb] = starts[tb].astype(I32)
        nxt = starts[tb + 1] if tb + 1 < nt else capf
        most = jnp.maximum(most, nxt - starts[tb])
    st_ref[0] = st.astype(I32)
    most_ref[0] = jnp.max(most, axis=0, keepdims=True).astype(I32)


def _topk(aff_t, cap):
    b, e, n = aff_t.shape
    nt = n // LANES
    return pl.pallas_call(
        functools.partial(_topk_kernel, cap=cap),
        grid=(b,),
        in_specs=[pl.BlockSpec((1, e, n), lambda i: (i, 0, 0))],
        out_specs=[pl.BlockSpec((1, e, n), lambda i: (i, 0, 0)),
                   pl.BlockSpec((1, e, n), lambda i: (i, 0, 0)),
                   pl.BlockSpec((1, e, nt), lambda i: (i, 0, 0)),
                   pl.BlockSpec((1, nt, e, 1), lambda i: (i, 0, 0, 0)),
                   pl.BlockSpec((1, 1, 1), lambda i: (i, 0, 0))],
        out_shape=[jax.ShapeDtypeStruct((b, e, n), I32),
                   jax.ShapeDtypeStruct((b, e, n), F32),
                   jax.ShapeDtypeStruct((b, e, nt), I32),
                   jax.ShapeDtypeStruct((b, nt, e, 1), I32),
                   jax.ShapeDtypeStruct((b, 1, 1), I32)],
        name="moe_topk",
        compiler_params=_cparams("arbitrary"),
    )(aff_t)


def _gather_kernel(st_ref, most_ref, selpos_ref, gsel_ref, stv_ref, h_ref, xs_ref, gs_ref, *, cap):
    bi = pl.program_id(0)
    ne = xs_ref.shape[0]
    nt = h_ref.shape[1] // LANES
    first_cols = pl.program_id(1) == 0
    xs_ref[...] = jnp.zeros_like(xs_ref)

    @pl.when(first_cols)
    def _():
        gs_ref[...] = jnp.zeros_like(gs_ref)

    def run(win, unroll, with_gate):
        riota = lax.broadcasted_iota(I32, (ne, win, LANES), 1)

        def body(tb, _):
            rows = pl.ds(pl.multiple_of(tb * LANES, LANES), LANES)
            slot = selpos_ref[0, :, pl.ds(tb, 1), :]
            p0v = jnp.minimum(jnp.bitwise_and(stv_ref[0, tb], -BF16_ROWS), cap - win)
            onehot = jnp.where(slot - p0v[:, :, None] == riota, 1.0, 0.0)
            res = _dot(onehot.reshape(ne * win, LANES).astype(BF16), h_ref[0, rows, :])
            if with_gate:
                gate = jnp.sum(onehot * gsel_ref[0, :, pl.ds(tb, 1), :], axis=-1, keepdims=True)
            for ei in range(ne):
                start = st_ref[(bi * ne + ei) * nt + tb]
                p0 = pl.multiple_of(jnp.minimum(jnp.bitwise_and(start, -BF16_ROWS), cap - win), BF16_ROWS)
                xs_ref[ei, 0, pl.ds(p0, win), :] += res[ei * win:(ei + 1) * win].astype(xs_ref.dtype)
                if with_gate:
                    gs_ref[ei, 0, pl.ds(p0, win), :] += gate[ei]
            return 0

        lax.fori_loop(0, nt, body, 0, unroll=unroll)

    fast = min(GATHER_FAST_WIN, cap)
    full = min(GATHER_WIN, cap)
    small = most_ref[bi] <= fast - BF16_ROWS
    for is_small, win, unroll in ((True, fast, min(2, nt)), (False, full, 1)):
        for with_gate in (True, False):
            @pl.when((small == is_small) & (first_cols == with_gate))
            def _(win=win, unroll=unroll, with_gate=with_gate):
                run(win, unroll, with_gate)


def _gather(st, most, selpos, gsel, stv, h2, cap):
    b, n, d = h2.shape
    e = selpos.shape[1]
    nt = n // LANES
    dh = d // 2
    grid_spec = pltpu.PrefetchScalarGridSpec(
        num_scalar_prefetch=2,
        grid=(b, d // dh),
        in_specs=[pl.BlockSpec((1, e, nt, LANES), lambda i, k, p, q: (i, 0, 0, 0)),
                  pl.BlockSpec((1, e, nt, LANES), lambda i, k, p, q: (i, 0, 0, 0)),
                  pl.BlockSpec((1, nt, e, 1), lambda i, k, p, q: (i, 0, 0, 0)),
                  pl.BlockSpec((1, n, dh), lambda i, k, p, q: (i, 0, k))],
        out_specs=[pl.BlockSpec((e, 1, cap, dh), lambda i, k, p, q: (0, i, 0, k)),
                   pl.BlockSpec((e, 1, cap, 1), lambda i, k, p, q: (0, i, 0, 0))],
    )
    return pl.pallas_call(
        functools.partial(_gather_kernel, cap=cap),
        grid_spec=grid_spec,
        out_shape=[jax.ShapeDtypeStruct((e, b, cap, d), BF16),
                   jax.ShapeDtypeStruct((e, b, cap, 1), F32)],
        name="moe_gather",
        compiler_params=_cparams("arbitrary", "arbitrary"),
    )(st.reshape(-1), most.reshape(-1), selpos.reshape(b, e, nt, LANES), gsel.reshape(b, e, nt, LANES), stv, h2)


def _ffn_kernel(*refs, ngroups, nsteps):
    xs = refs[:ngroups]
    gs = refs[ngroups:2 * ngroups]
    wg_ref, wu_ref, wd_ref = refs[2 * ngroups:2 * ngroups + 3]
    ys = refs[2 * ngroups + 3:3 * ngroups + 3]
    accs = refs[3 * ngroups + 3:]
    f = pl.program_id(1)
    last = nsteps - 1

    def step(first, final):
        wg = wu = wd = None
        for x_ref, g_ref, y_ref, acc_ref in zip(xs, gs, ys, accs):
            rows = x_ref.shape[1]
            tr = min(ROW_TILE, rows)
            for r0 in range(0, rows, tr):
                sl = slice(r0, r0 + tr)
                x = x_ref[0, sl, :]
                wg = wg_ref[0, 0].astype(BF16) if wg is None else wg
                g = _dot(x, wg)
                wu = wu_ref[0, 0].astype(BF16) if wu is None else wu
                u = _dot(x, wu)
                wd = wd_ref[0, 0].astype(BF16) if wd is None else wd
                part = _dot((g * _sigmoid(g) * u).astype(BF16), wd)
                total = part if first else acc_ref[sl, :] + part
                if final:
                    y_ref[0, sl, :] = (total * g_ref[0, sl, :]).astype(y_ref.dtype)
                else:
                    acc_ref[sl, :] = total

    if nsteps == 1:
        step(True, True)
        return
    variants = [(True, False, f == 0), (False, True, f == last)]
    if nsteps > 2:
        variants.append((False, False, (f > 0) & (f < last)))
    for first, final, cond in variants:
        @pl.when(cond)
        def _(first=first, final=final):
            step(first, final)


def _ffn(xs_list, gs_list, wg, wu, wd, layer):
    _, e, d, ff = wg.shape
    tf = min(FF_TILE, ff)
    ng = len(xs_list)
    in_specs = [pl.BlockSpec((1, x.shape[1], d), lambda i, j: (i, 0, 0)) for x in xs_list]
    in_specs += [pl.BlockSpec((1, x.shape[1], 1), lambda i, j: (i, 0, 0)) for x in xs_list]
    in_specs += [pl.BlockSpec((1, 1, d, tf), lambda i, j: (layer, i, 0, j)),
                 pl.BlockSpec((1, 1, d, tf), lambda i, j: (layer, i, 0, j)),
                 pl.BlockSpec((1, 1, tf, d), lambda i, j: (layer, i, j, 0))]
    return pl.pallas_call(
        functools.partial(_ffn_kernel, ngroups=ng, nsteps=ff // tf),
        grid=(e, ff // tf),
        in_specs=in_specs,
        out_specs=[pl.BlockSpec((1, x.shape[1], d), lambda i, j: (i, 0, 0)) for x in xs_list],
        out_shape=[jax.ShapeDtypeStruct(x.shape, BF16) for x in xs_list],
        scratch_shapes=[pltpu.VMEM((x.shape[1], d), F32) for x in xs_list],
        name="moe_ffn",
        compiler_params=_cparams("arbitrary", "arbitrary"),
    )(*xs_list, *gs_list, wg, wu, wd)


def _combine_kernel(st_ref, most_ref, selpos_ref, stv_ref, y_ref, x_ref, mod_ref, fg_ref, o_ref, ystk_ref, *,
                    cap, nt_total, final):
    bi = pl.program_id(0)
    tg = pl.program_id(2)
    ne = y_ref.shape[0]
    tm = x_ref.shape[1]
    mgate = mod_ref[0][5:6]

    def emit(rows, acc):
        res = x_ref[0, rows, :] + mgate * acc
        o_ref[0, rows, :] = _rms(res, fg_ref[...]) if final else res

    fwin = min(SCATTER_FAST_WIN, cap)
    swin = min(SCATTER_WIN, cap)

    def window_start(ei, tb, win):
        start = st_ref[(bi * ne + ei) * nt_total + tb]
        return pl.multiple_of(jnp.minimum(jnp.bitwise_and(start, -BF16_ROWS), cap - win), BF16_ROWS)

    def fast_body(t, _):
        tb = tg * (tm // LANES) + t
        rows = pl.ds(pl.multiple_of(t * LANES, LANES), LANES)
        for ei in range(ne):
            p0 = window_start(ei, tb, fwin)
            ystk_ref[ei * fwin:(ei + 1) * fwin, :] = y_ref[ei, 0, pl.ds(p0, fwin), :]
        slot = selpos_ref[0, :, pl.ds(tb, 1), :]
        p0v = jnp.minimum(jnp.bitwise_and(stv_ref[0, tb], -BF16_ROWS), cap - fwin)
        riota = lax.broadcasted_iota(I32, (ne, fwin, LANES), 1)
        onehot_t = jnp.where(slot - p0v[:, :, None] == riota, 1.0, 0.0)
        acc = _tn(onehot_t.reshape(ne * fwin, LANES).astype(BF16), ystk_ref[...])
        emit(rows, acc)
        return 0

    def full_body(t, _):
        tb = tg * (tm // LANES) + t
        rows = pl.ds(pl.multiple_of(t * LANES, LANES), LANES)
        riota = lax.broadcasted_iota(I32, (swin, LANES), 0)
        acc = jnp.zeros((LANES, x_ref.shape[2]), F32)
        for ei in range(ne):
            p0 = window_start(ei, tb, swin)
            slot = selpos_ref[0, ei, pl.ds(tb, 1), :]
            onehot_t = jnp.where(slot - p0 == riota, 1.0, 0.0).astype(BF16)
            acc = acc + _tn(onehot_t, y_ref[ei, 0, pl.ds(p0, swin), :])
        emit(rows, acc)
        return 0

    most = most_ref[bi]

    @pl.when(most <= fwin - BF16_ROWS)
    def _():
        lax.fori_loop(0, tm // LANES, fast_body, 0)

    @pl.when(most > fwin - BF16_ROWS)
    def _():
        lax.fori_loop(0, tm // LANES, full_body, 0)


def _combine(st, most, selpos, stv, y, x, mod, cap, final_g=None):
    b, n, d = x.shape
    e = y.shape[0]
    tm = min(ROW_TILE, n)
    dh = d
    nt = n // LANES
    final = final_g is not None
    fg = final_g if final else jnp.ones((1, d), F32)
    grid_spec = pltpu.PrefetchScalarGridSpec(
        num_scalar_prefetch=2,
        grid=(b, d // dh, n // tm),
        in_specs=[pl.BlockSpec((1, e, nt, LANES), lambda i, k, j, p, q: (i, 0, 0, 0)),
                  pl.BlockSpec((1, nt, e, 1), lambda i, k, j, p, q: (i, 0, 0, 0)),
                  pl.BlockSpec((e, 1, cap, dh), lambda i, k, j, p, q: (0, i, 0, k)),
                  pl.BlockSpec((1, tm, dh), lambda i, k, j, p, q: (i, j, k)),
                  pl.BlockSpec((1, 6, dh), lambda i, k, j, p, q: (i, 0, k)),
                  pl.BlockSpec((1, dh), lambda i, k, j, p, q: (0, k))],
        out_specs=pl.BlockSpec((1, tm, dh), lambda i, k, j, p, q: (i, j, k)),
        scratch_shapes=[pltpu.VMEM((e * min(SCATTER_FAST_WIN, cap), dh), BF16)],
    )
    return pl.pallas_call(
        functools.partial(_combine_kernel, cap=cap, nt_total=nt, final=final),
        grid_spec=grid_spec,
        out_shape=jax.ShapeDtypeStruct((b, n, d), F32),
        name="moe_combine",
        compiler_params=_cparams("arbitrary", "arbitrary", "arbitrary"),
    )(st.reshape(-1), most.reshape(-1), selpos.reshape(b, e, nt, LANES), stv, y, x, mod, fg)


def _rot_cols(w):
    q = MLA_ROPE // 4
    return jnp.concatenate([-w[:, q:2 * q], w[:, :q], -w[:, 3 * q:], w[:, 2 * q:3 * q]], axis=1)


def _prep_even(w_in, wq_b, wkv_b):
    kpe = w_in[:, -MLA_ROPE:]
    win = jnp.concatenate([w_in, _rot_cols(kpe)], axis=1).astype(BF16)
    qh = wq_b.reshape(MLA_Q_RANK, MLA_HEADS, MLA_NOPE + MLA_ROPE)
    nope = [qh[:, h, :MLA_NOPE] for h in range(MLA_HEADS)]
    pe = [jnp.concatenate([qh[:, h, MLA_NOPE:], _rot_cols(qh[:, h, MLA_NOPE:])], axis=1) for h in range(MLA_HEADS)]
    wq = jnp.concatenate(nope + pe, axis=1).astype(BF16)
    kvh = wkv_b.reshape(MLA_KV_RANK, MLA_HEADS, MLA_NOPE + MLA_V)
    wkv = jnp.concatenate([kvh[:, h, :MLA_NOPE] for h in range(MLA_HEADS)]
                          + [kvh[:, h, MLA_NOPE:] for h in range(MLA_HEADS)], axis=1).astype(BF16)
    return win, wq, wkv


def _rope_tables(n):
    rows = n // GRID_W
    row = jnp.repeat(jnp.arange(rows, dtype=F32), GRID_W)
    col = jnp.tile(jnp.arange(GRID_W, dtype=F32), rows)
    half = MLA_ROPE // 2
    inv = 1.0 / (ROPE_THETA ** (jnp.arange(0, half, 2, dtype=F32) / half))
    ar = row[:, None] * inv[None, :]
    ac = col[:, None] * inv[None, :]
    ang = jnp.concatenate([ar, ar, ac, ac], axis=-1)
    z = jnp.zeros((n, LANES - MLA_ROPE), F32)
    return jnp.concatenate([jnp.cos(ang), z], axis=1), jnp.concatenate([jnp.sin(ang), z], axis=1)


def _moe(streams, wg, wu, wd, layer, final_g=None):
    routed = []
    for xx, mm, h2, aff_t in streams:
        b, n, d = xx.shape
        e = aff_t.shape[1]
        cap = EC_FACTOR * n // e
        selpos, gsel, st, stv, most = _topk(aff_t, cap)
        xs, gs = _gather(st, most, selpos, gsel, stv, h2, cap)
        routed.append((st, most, selpos, stv, cap, xs, gs))
    flat = lambda a: a.reshape(a.shape[0], -1, a.shape[3])
    ys = _ffn([flat(r[5]) for r in routed], [flat(r[6]) for r in routed], wg, wu, wd, layer)
    outs = []
    for si, ((xx, mm, _, _), (st, most, selpos, stv, cap, xs, gs), y) in enumerate(zip(streams, routed, ys)):
        outs.append(_combine(st, most, selpos, stv, y.reshape(xs.shape), xx, mm, cap,
                             final_g if si == 0 else None))
    return outs


def kernel(x, c, ctx, c_ctx, ada_w, ada_b, norm1_g, norm2_g, w_in, hg_lb, hg_norm_g, mla_qn_g, mla_wq_b,
           mla_kvn_g, mla_wkv_b, w_out, pool_w, pool_scale, router_w, exp_wg, exp_wu, exp_wd, final_g):
    b, n, d = x.shape
    nctx = ctx.shape[1]
    depth = ada_w.shape[0]
    n_even = w_in.shape[0]
    assert b + 1 <= SUBLANES and n % LANES == 0 and nctx % LANES == 0 and n % GRID_W == 0

    cc = jnp.zeros((SUBLANES, d), F32).at[:b].set(c).at[b].set(c_ctx)
    mods = _ada(cc, ada_w, ada_b)
    mod_lat = [mods[l, :b].reshape(b, 6, d) for l in range(depth)]
    mod_ctx = [jnp.broadcast_to(mods[l, b].reshape(1, 6, d), (b, 6, d)) for l in range(depth)]

    lb_all = jnp.cumsum(jax.nn.softmax(hg_lb.astype(F32), axis=0), axis=0)
    lb_all = lb_all - lb_all[:1]
    cos1, sin1 = _rope_tables(n)
    cos_c = jnp.concatenate([jnp.ones((nctx, MLA_ROPE), F32), jnp.zeros((nctx, LANES - MLA_ROPE), F32)], axis=1)
    sin_c = jnp.zeros((nctx, LANES), F32)

    last_reader = 2 * (n_even - 1)
    row = lambda v: v.reshape(1, -1)
    x_lat, x_ctx = x, ctx
    for l in range(depth):
        j = l // 2
        ctx_in = l <= last_reader
        ctx_out = l < last_reader
        g2, rt = row(norm2_g[l]), router_w[l].T
        if l % 2 == 0:
            win, wq, wkv = _prep_even(w_in[j], mla_wq_b[j], mla_wkv_b[j])
            lb = lb_all[j]
            lbp = jnp.stack([lb[0], 1.0 - lb[0], lb[1], 1.0 - lb[1]])
            common = (row(norm1_g[l]), win, row(mla_qn_g[j]), wq, row(mla_kvn_g[j]), wkv)
            hg_l, q_l, k_l, v_l = _in_proj(x_lat, mod_lat[l], *common, cos1, sin1)
            hg_c = None
            ks, vs = [k_l], [v_l]
            if ctx_in:
                hg_c, q_c, k_c, v_c = _in_proj(x_ctx, mod_ctx[l], *common, cos_c, sin_c)
                ks, vs = [k_c, k_l], [v_c, v_l]
            hgo_l, hgo_c = _hgrn(hg_l, hg_c, lbp, row(hg_norm_g[j]), ctx_out)
            mla_l = _attention(q_l, ks, vs)
            wo = w_out[j].astype(BF16)
            streams = [(mod_lat[l],) + tuple(_out_proj(hgo_l, mla_l, x_lat, mod_lat[l], wo, g2, rt))]
            if ctx_out:
                mla_c = _attention(q_c, [k_c], [v_c])
                streams.append((mod_ctx[l],) + tuple(_out_proj(hgo_c, mla_c, x_ctx, mod_ctx[l], wo, g2, rt)))
        else:
            pw = pool_w[j].astype(BF16)
            pool_args = (row(norm1_g[l]), pw, row(pool_scale[j]), g2, rt)
            streams = [(mod_lat[l],) + tuple(_pool(x_lat, mod_lat[l], *pool_args))]
            if ctx_out:
                streams.append((mod_ctx[l],) + tuple(_pool(x_ctx, mod_ctx[l], *pool_args)))
        outs = _moe([(xn, mm, h2, aff) for mm, xn, h2, aff in streams], exp_wg, exp_wu, exp_wd, l,
                    final_g=row(final_g) if l == depth - 1 else None)
        x_lat = outs[0]
        if ctx_out:
            x_ctx = outs[1]
    return x_lat
```

```python
import functools

import numpy as np
import jax
import jax.numpy as jnp
from jax import lax
from jax.experimental import pallas as pl
from jax.experimental.pallas import tpu as pltpu

F32 = jnp.float32
BF16 = jnp.bfloat16
I32 = jnp.int32

EPS = 1e-6
GRID_W = 64
ROPE_THETA = 10000.0
HG_HEADS = 4
HG_D = 128
MLA_HEADS = 4
MLA_Q_RANK = 256
MLA_KV_RANK = 128
MLA_NOPE = 128
MLA_ROPE = 64
MLA_V = 128
MLA_SCALE = (MLA_NOPE + MLA_ROPE) ** -0.5
POOL_WINDOWS = (2, 4, 8, 16)
EC_FACTOR = 2
HG_W = HG_HEADS * HG_D

LANES = 128
CHUNK = 64
HG_HEADS_PER_STEP = 1
HG_CHAINS = 32
POOL_HALO = 8
SUBLANES = 8
BF16_ROWS = 16
GATHER_WIN = LANES + BF16_ROWS
GATHER_FAST_WIN = 48
SCATTER_WIN = 256
SCATTER_FAST_WIN = 64
ATTN_Q_SUB = 512
ATTN_Q_TILE = 2048
LOG2E = 1.4426950408889634
ROW_TILE = 512
FF_TILE = 512
ADA_COL_TILES = 4
V7X_VMEM_BYTES = 64 * 1024 * 1024
VMEM_LIMIT_V7X = V7X_VMEM_BYTES - 8 * 1024 * 1024


def _cparams(*sem):
    return pltpu.CompilerParams(dimension_semantics=sem, vmem_limit_bytes=VMEM_LIMIT_V7X)


def _nt(a, b):
    return lax.dot_general(a, b, (((1,), (1,)), ((), ())), preferred_element_type=F32)


def _tn(a, b):
    return lax.dot_general(a, b, (((0,), (0,)), ((), ())), preferred_element_type=F32)


def _dot(a, b):
    return jnp.dot(a, b, preferred_element_type=F32)


def _sigmoid(x):
    return 1.0 / (1.0 + jnp.exp(-x))


def _rms(x, g):
    return x * lax.rsqrt(jnp.mean(x * x, axis=-1, keepdims=True) + EPS) * g


def _modulate(x, g, shift, scale):
    return _rms(x, g) * (1.0 + scale) + shift


def _ada_kernel(c_ref, w_ref, b_ref, o_ref):
    c = c_ref[...]
    s = (c * _sigmoid(c)).astype(BF16)
    o_ref[0] = _dot(s, w_ref[0].astype(BF16)) + b_ref[0]


def _ada(cc, ada_w, ada_b):
    depth, d, d6 = ada_w.shape
    tn = d6 // ADA_COL_TILES
    rows = cc.shape[0]
    return pl.pallas_call(
        _ada_kernel,
        grid=(depth, ADA_COL_TILES),
        in_specs=[pl.BlockSpec((rows, d), lambda l, j: (0, 0)),
                  pl.BlockSpec((1, d, tn), lambda l, j: (l, 0, j)),
                  pl.BlockSpec((1, 1, tn), lambda l, j: (l, 0, j))],
        out_specs=pl.BlockSpec((1, rows, tn), lambda l, j: (l, 0, j)),
        out_shape=jax.ShapeDtypeStruct((depth, rows, d6), F32),
        name="ada_mod",
        compiler_params=_cparams("arbitrary", "arbitrary"),
    )(cc, ada_w, ada_b.reshape(depth, 1, d6))


def _in_kernel(x_ref, mod_ref, g1_ref, win_ref, qng_ref, wq_ref, kvg_ref, wkv_ref, cs1_ref, cs2_ref,
               hg_ref, q_ref, k_ref, vt_ref):
    m = mod_ref[0]
    h = _modulate(x_ref[0], g1_ref[...], m[0:1], m[1:2]).astype(BF16)
    proj = _dot(h, win_ref[...])
    o = 5 * HG_W
    hg_ref[0] = proj[:, :o]
    qa = proj[:, o:o + MLA_Q_RANK]
    kva = proj[:, o + MLA_Q_RANK:o + MLA_Q_RANK + MLA_KV_RANK]
    kp = proj[:, o + MLA_Q_RANK + MLA_KV_RANK:]
    qup = _dot(_rms(qa, qng_ref[...]).astype(BF16), wq_ref[...])
    kvup = _dot(_rms(kva, kvg_ref[...]).astype(BF16), wkv_ref[...])
    cs1 = cs1_ref[...]
    cs2 = cs2_ref[...]
    half = LANES // 2
    kpe = (kp * cs1 + pltpu.roll(kp, half, axis=1) * cs2).astype(BF16)
    nn = MLA_HEADS * MLA_NOPE
    qs = MLA_SCALE * LOG2E
    for hd in range(MLA_HEADS):
        p = qup[:, nn + LANES * hd:nn + LANES * (hd + 1)]
        qpe = p * cs1 + pltpu.roll(p, half, axis=1) * cs2
        q_ref[0, :, 2 * LANES * hd:2 * LANES * hd + LANES] = (qup[:, LANES * hd:LANES * (hd + 1)] * qs).astype(BF16)
        q_ref[0, :, 2 * LANES * hd + LANES:2 * LANES * (hd + 1)] = (qpe * qs).astype(BF16)
        k_ref[0, :, 2 * LANES * hd:2 * LANES * hd + LANES] = kvup[:, LANES * hd:LANES * (hd + 1)].astype(BF16)
        k_ref[0, :, 2 * LANES * hd + LANES:2 * LANES * (hd + 1)] = kpe
    vt_ref[0] = kvup[:, nn:].T.astype(BF16)


def _in_proj(x, mod, g1, win, qng, wq, kvg, wkv, cs1, cs2):
    b, n, d = x.shape
    tm = min(ROW_TILE, n)
    wcols = win.shape[1]
    full = lambda shape: pl.BlockSpec(shape, lambda i, j: (0,) * len(shape))
    return pl.pallas_call(
        _in_kernel,
        grid=(b, n // tm),
        in_specs=[pl.BlockSpec((1, tm, d), lambda i, j: (i, j, 0)),
                  pl.BlockSpec((1, 6, d), lambda i, j: (i, 0, 0)),
                  full((1, d)), full((d, wcols)), full((1, MLA_Q_RANK)), full(wq.shape),
                  full((1, MLA_KV_RANK)), full(wkv.shape),
                  pl.BlockSpec((tm, LANES), lambda i, j: (j, 0)),
                  pl.BlockSpec((tm, LANES), lambda i, j: (j, 0))],
        out_specs=[pl.BlockSpec((1, tm, 5 * HG_W), lambda i, j: (i, j, 0)),
                   pl.BlockSpec((1, tm, 2 * LANES * MLA_HEADS), lambda i, j: (i, j, 0)),
                   pl.BlockSpec((1, tm, 2 * LANES * MLA_HEADS), lambda i, j: (i, j, 0)),
                   pl.BlockSpec((1, MLA_HEADS * MLA_V, tm), lambda i, j: (i, 0, j))],
        out_shape=[jax.ShapeDtypeStruct((b, n, 5 * HG_W), F32),
                   jax.ShapeDtypeStruct((b, n, 2 * LANES * MLA_HEADS), BF16),
                   jax.ShapeDtypeStruct((b, n, 2 * LANES * MLA_HEADS), BF16),
                   jax.ShapeDtypeStruct((b, MLA_HEADS * MLA_V, n), BF16)],
        name="in_proj",
        compiler_params=_cparams("arbitrary", "arbitrary"),
    )(x, mod, g1, win, qng, wq, kvg, wkv, cs1, cs2)


def _attn_kernel(*refs, nparts):
    q_ref = refs[0]
    k_refs = refs[1:1 + nparts]
    v_refs = refs[1 + nparts:1 + 2 * nparts]
    o_ref = refs[1 + 2 * nparts]
    tq = q_ref.shape[1]
    sub = min(ATTN_Q_SUB, tq)
    qk = lambda q0: [_nt(k[0], q_ref[0, q0:q0 + sub, :]) for k in k_refs]
    nxt = qk(0)
    for q0 in range(0, tq, sub):
        s = nxt
        if q0 + sub < tq:
            nxt = qk(q0 + sub)
        m = functools.reduce(jnp.maximum, [jnp.max(x, axis=0, keepdims=True) for x in s])
        p = [jnp.exp2(x - m) for x in s]
        den = functools.reduce(jnp.add, [jnp.sum(x, axis=0, keepdims=True) for x in p])
        acc = functools.reduce(jnp.add, [_dot(vt[0], x.astype(BF16)) for vt, x in zip(v_refs, p)])
        o_ref[0, q0:q0 + sub, :] = (acc / den).T.astype(o_ref.dtype)


def _attention(q, ks, vts):
    b, n, _ = q.shape
    tq = min(ATTN_Q_TILE, n)
    nparts = len(ks)
    in_specs = [pl.BlockSpec((1, tq, 2 * LANES), lambda i, h, j: (i, j, h))]
    in_specs += [pl.BlockSpec((1, k.shape[1], 2 * LANES), lambda i, h, j: (i, 0, h)) for k in ks]
    in_specs += [pl.BlockSpec((1, MLA_V, v.shape[2]), lambda i, h, j: (i, h, 0)) for v in vts]
    return pl.pallas_call(
        functools.partial(_attn_kernel, nparts=nparts),
        grid=(b, MLA_HEADS, n // tq),
        in_specs=in_specs,
        out_specs=pl.BlockSpec((1, tq, MLA_V), lambda i, h, j: (i, j, h)),
        out_shape=jax.ShapeDtypeStruct((b, n, MLA_HEADS * MLA_V), BF16),
        name="mla_attention",
        compiler_params=_cparams("arbitrary", "arbitrary", "arbitrary"),
    )(q, *ks, *vts)


_HG_HALVES = (1, 2, 4, 8, 16, 32)


def _hgrn_masks():
    t = np.arange(CHUNK)[:, None]
    s = np.arange(CHUNK)[None, :]
    masks_f = [(t == s)]
    for h in _HG_HALVES:
        masks_f.append((t // (2 * h) == s // (2 * h)) & (t % (2 * h) >= h) & (s % (2 * h) < h))
    masks_f = np.stack(masks_f).astype(np.float32)
    return masks_f, np.transpose(masks_f, (0, 2, 1))


def _hg_block_decays(f, fw):
    c = CHUNK
    pre, suf = [f], [jnp.ones_like(f)]
    h = 1
    while h < c:
        blk = 2 * h
        src_early = h - 1 if fw else h
        src_late = blk - 1 if fw else 0
        p = pre[-1]
        if blk <= SUBLANES:
            p3 = p.reshape(c // SUBLANES, SUBLANES, HG_D)
            r = lax.broadcasted_iota(I32, p3.shape, 1)
            tot_e = jnp.ones_like(p3)
            tot_l = jnp.ones_like(p3)
            for kb in range(SUBLANES // blk):
                inblk = jnp.right_shift(r, blk.bit_length() - 1) == kb
                tot_e = jnp.where(inblk, p3[:, kb * blk + src_early:kb * blk + src_early + 1, :], tot_e)
                tot_l = jnp.where(inblk, p3[:, kb * blk + src_late:kb * blk + src_late + 1, :], tot_l)
        else:
            p3 = p.reshape(c // blk, blk, HG_D)
            r = lax.broadcasted_iota(I32, p3.shape, 1)
            tot_e = jnp.broadcast_to(p3[:, src_early:src_early + 1, :], p3.shape)
            tot_l = jnp.broadcast_to(p3[:, src_late:src_late + 1, :], p3.shape)
        in_first = jnp.bitwise_and(r, blk - 1) < h
        late = jnp.logical_not(in_first) if fw else in_first
        s3 = suf[-1].reshape(p3.shape)
        pre.append(jnp.where(late, p3 * tot_e, p3).reshape(c, HG_D))
        suf.append(jnp.where(late, s3, s3 * tot_l).reshape(c, HG_D))
        h = blk
    return pre, suf


def _hg_chunk(q, z, v, lb, onemlb, masks, st, fw, want_out):
    t = jnp.exp(-jnp.abs(z))
    r = 1.0 / (1.0 + t)
    pos = z >= 0.0
    f = lb + onemlb * jnp.where(pos, r, t * r)
    k = onemlb * jnp.where(pos, t * r, r)
    pre, suf = _hg_block_decays(f, fw)
    vb = v.astype(BF16)
    whole = pre[-1]
    kdl = (k * suf[-1]).astype(BF16)
    last = CHUNK - 1 if fw else 0
    dec = whole[last:last + 1]
    out = None
    if want_out:
        kb = k.astype(BF16)
        att = masks[0] * _nt(q.astype(BF16), kb)
        for i in range(len(_HG_HALVES)):
            a = (q * pre[i]).astype(BF16)
            bm = kb if i == 0 else (k * suf[i]).astype(BF16)
            att = att + masks[1 + i] * _nt(a, bm)
        out = _dot(att.astype(BF16), vb) + _nt((q * whole).astype(BF16), st.astype(BF16))
    st_new = st * dec + _tn(vb, kdl)
    return out, st_new


def _hgrn_kernel(*refs, has_ctx, ctx_out):
    it = iter(refs)
    q_ref, zf_ref, zb_ref, v_ref, g_ref = (next(it) for _ in range(5))
    if has_ctx:
        qc_ref, zfc_ref, zbc_ref, vc_ref, gc_ref = (next(it) for _ in range(5))
    lb_ref, gn_ref, mkf_ref, mkb_ref = (next(it) for _ in range(4))
    o_ref = next(it)
    oc_ref = next(it) if ctx_out else None
    of_s, ob_s = next(it), next(it)
    if ctx_out:
        ocf_s, ocb_s = next(it), next(it)

    lbp = lb_ref[...]
    masks = (mkf_ref[...], mkb_ref[...])
    zrefs = (zf_ref, zb_ref)

    def seq_pass(q_r, z_rs, v_r, o_ss, n, states, want_out):
        nc = n // CHUNK

        def body(ci, carry):
            new = []
            for hd in range(HG_HEADS_PER_STEP):
                cols = slice(hd * HG_D, (hd + 1) * HG_D)
                for d in range(2):
                    cidx = ci if d == 0 else nc - 1 - ci
                    off = pl.multiple_of(cidx * CHUNK, CHUNK)
                    rows = pl.ds(off, CHUNK)
                    qv = q_r[0, rows, cols] if want_out else None
                    out, st = _hg_chunk(qv, z_rs[d][0, rows, cols], v_r[0, rows, cols],
                                        lbp[2 * d:2 * d + 1, cols], lbp[2 * d + 1:2 * d + 2, cols],
                                        masks[d], carry[2 * hd + d], d == 0, want_out)
                    if want_out:
                        o_ss[d][rows, cols] = out
                    new.append(st)
            return tuple(new)

        return lax.fori_loop(0, nc, body, states, unroll=HG_CHAINS // (2 * HG_HEADS_PER_STEP))

    def readout(o_ss, g_r, out_r, n):
        tr = min(ROW_TILE, n)

        def body(i, _):
            rows = pl.ds(pl.multiple_of(i * tr, tr), tr)
            for hd in range(HG_HEADS_PER_STEP):
                cols = slice(hd * HG_D, (hd + 1) * HG_D)
                o = o_ss[0][rows, cols] + o_ss[1][rows, cols]
                g = g_r[0, rows, cols]
                out_r[0, rows, cols] = (_rms(o, gn_ref[...]) * (g * _sigmoid(g))).astype(out_r.dtype)
            return 0

        lax.fori_loop(0, n // tr, body, 0)

    zero = jnp.zeros((HG_D, HG_D), F32)
    states = (zero,) * (2 * HG_HEADS_PER_STEP)
    if has_ctx:
        nctx = vc_ref.shape[1]
        states = seq_pass(qc_ref, (zfc_ref, zbc_ref), vc_ref, (ocf_s, ocb_s) if ctx_out else None,
                          nctx, states, ctx_out)
        if ctx_out:
            readout((ocf_s, ocb_s), gc_ref, oc_ref, nctx)
    n = v_ref.shape[1]
    seq_pass(q_ref, zrefs, v_ref, (of_s, ob_s), n, states, True)
    readout((of_s, ob_s), g_ref, o_ref, n)


def _hgrn(hg, hg_ctx, lbp, gn, ctx_out):
    b, n, _ = hg.shape
    has_ctx = hg_ctx is not None
    consts = [jnp.asarray(a, F32) for a in _hgrn_masks()]

    hw = HG_HEADS_PER_STEP * HG_D
    groups = HG_HEADS // HG_HEADS_PER_STEP

    def part(nn, k):
        return pl.BlockSpec((1, nn, hw), lambda i, h, k=k: (i, 0, k * groups + h))

    in_specs = [part(n, k) for k in range(5)]
    args = [hg] * 5
    if has_ctx:
        nctx = hg_ctx.shape[1]
        in_specs += [part(nctx, k) for k in range(5)]
        args += [hg_ctx] * 5
    in_specs += [pl.BlockSpec((4, hw), lambda i, h: (0, h)),
                 pl.BlockSpec((1, HG_D), lambda i, h: (0, 0))]
    in_specs += [pl.BlockSpec(a.shape, lambda i, h, nd=a.ndim: (0,) * nd) for a in consts]
    args += [lbp, gn] + consts
    out_specs = [pl.BlockSpec((1, n, hw), lambda i, h: (i, 0, h))]
    out_shape = [jax.ShapeDtypeStruct((b, n, HG_W), BF16)]
    scratch = [pltpu.VMEM((n, hw), F32), pltpu.VMEM((n, hw), F32)]
    if ctx_out:
        out_specs.append(pl.BlockSpec((1, nctx, hw), lambda i, h: (i, 0, h)))
        out_shape.append(jax.ShapeDtypeStruct((b, nctx, HG_W), BF16))
        scratch += [pltpu.VMEM((nctx, hw), F32), pltpu.VMEM((nctx, hw), F32)]
    res = pl.pallas_call(
        functools.partial(_hgrn_kernel, has_ctx=has_ctx, ctx_out=ctx_out),
        grid=(b, groups),
        in_specs=in_specs, out_specs=out_specs, out_shape=out_shape, scratch_shapes=scratch,
        name="hgrn_scan",
        compiler_params=_cparams("arbitrary", "arbitrary"),
    )(*args)
    return (res[0], res[1]) if ctx_out else (res[0], None)


def _route_tile(x, m, g2, rt):
    h = _modulate(x, g2, m[3:4], m[4:5])
    hb = h.astype(BF16)
    hlo = (h - hb.astype(F32)).astype(BF16)
    rhi = rt.astype(BF16)
    rlo = (rt - rhi.astype(F32)).astype(BF16)
    logits = _nt(rhi, hb) + (_nt(rhi, hlo) + _nt(rlo, hb))
    ex = jnp.exp(logits - jnp.max(logits, axis=0, keepdims=True))
    return hb, ex / jnp.sum(ex, axis=0, keepdims=True)


def _mixer_out_specs(b, n, d, e, tm):
    specs = [pl.BlockSpec((1, tm, d), lambda i, j: (i, j, 0)),
             pl.BlockSpec((1, tm, d), lambda i, j: (i, j, 0)),
             pl.BlockSpec((1, e, tm), lambda i, j: (i, 0, j))]
    shapes = [jax.ShapeDtypeStruct((b, n, d), F32),
              jax.ShapeDtypeStruct((b, n, d), BF16),
              jax.ShapeDtypeStruct((b, e, n), F32)]
    return specs, shapes


def _out_kernel(hg_ref, mla_ref, x_ref, mod_ref, w_ref, g2_ref, rt_ref, o_ref, h_ref, aff_ref):
    w = w_ref[...]
    m = mod_ref[0]
    y = _dot(hg_ref[0], w[:HG_W]) + _dot(mla_ref[0], w[HG_W:])
    xn = x_ref[0] + m[2:3] * y
    o_ref[0] = xn
    h_ref[0], aff_ref[0] = _route_tile(xn, m, g2_ref[...], rt_ref[...])


def _out_proj(hgo, mla, x, mod, w, g2, rt):
    b, n, d = x.shape
    e = rt.shape[0]
    tm = min(ROW_TILE, n)
    out_specs, out_shape = _mixer_out_specs(b, n, d, e, tm)
    return pl.pallas_call(
        _out_kernel,
        grid=(b, n // tm),
        in_specs=[pl.BlockSpec((1, tm, HG_W), lambda i, j: (i, j, 0)),
                  pl.BlockSpec((1, tm, MLA_HEADS * MLA_V), lambda i, j: (i, j, 0)),
                  pl.BlockSpec((1, tm, d), lambda i, j: (i, j, 0)),
                  pl.BlockSpec((1, 6, d), lambda i, j: (i, 0, 0)),
                  pl.BlockSpec(w.shape, lambda i, j: (0, 0)),
                  pl.BlockSpec((1, d), lambda i, j: (0, 0)),
                  pl.BlockSpec((e, d), lambda i, j: (0, 0))],
        out_specs=out_specs, out_shape=out_shape,
        name="out_proj",
        compiler_params=_cparams("arbitrary", "arbitrary"),
    )(hgo, mla, x, mod, w, g2, rt)


def _pool_kernel(x_ref, prev_ref, next_ref, mod_ref, g1_ref, pw_ref, ps_ref, g2_ref, rt_ref,
                 o_ref, h_ref, aff_ref, hs_ref, *, n):
    tm = x_ref.shape[1]
    d = x_ref.shape[2]
    gw = d // len(POOL_WINDOWS)
    m = mod_ref[0]
    base = pl.program_id(1) * tm
    xe = jnp.concatenate([prev_ref[0], x_ref[0], next_ref[0]], axis=0)
    he = _modulate(xe, g1_ref[...], m[0:1], m[1:2])
    gidx = base - POOL_HALO + lax.broadcasted_iota(I32, (tm + 2 * POOL_HALO, 1), 0)
    hs_ref[...] = jnp.where((gidx >= 0) & (gidx < n), he, 0.0)
    t = base + lax.broadcasted_iota(I32, (tm, 1), 0)
    ps = ps_ref[...]
    for gi, w in enumerate(POOL_WINDOWS):
        cols = slice(gi * gw, (gi + 1) * gw)
        acc = hs_ref[pl.ds(POOL_HALO - w // 2, tm), cols]
        for j in range(1 - w // 2, w // 2):
            acc = acc + hs_ref[pl.ds(POOL_HALO + j, tm), cols]
        lo = jnp.clip(t - w // 2, 0, n - 1)
        hi = jnp.clip(t + w // 2 - 1, 0, n - 1)
        pooled = acc / (hi - lo + 1).astype(F32) - hs_ref[pl.ds(POOL_HALO, tm), cols]
        y = _dot(pooled.astype(BF16), pw_ref[gi]) * ps[:, cols]
        o_ref[0, :, cols] = x_ref[0, :, cols] + m[2:3, cols] * y
    h_ref[0], aff_ref[0] = _route_tile(o_ref[0], m, g2_ref[...], rt_ref[...])


def _pool(x, mod, g1, pw, ps, g2, rt):
    b, n, d = x.shape
    e = rt.shape[0]
    tm = min(ROW_TILE, n)
    nb = tm // POOL_HALO
    last = n // POOL_HALO - 1
    out_specs, out_shape = _mixer_out_specs(b, n, d, e, tm)
    return pl.pallas_call(
        functools.partial(_pool_kernel, n=n),
        grid=(b, n // tm),
        in_specs=[pl.BlockSpec((1, tm, d), lambda i, j: (i, j, 0)),
                  pl.BlockSpec((1, POOL_HALO, d), lambda i, j: (i, jnp.maximum(j * nb - 1, 0), 0)),
                  pl.BlockSpec((1, POOL_HALO, d), lambda i, j: (i, jnp.minimum((j + 1) * nb, last), 0)),
                  pl.BlockSpec((1, 6, d), lambda i, j: (i, 0, 0)),
                  pl.BlockSpec((1, d), lambda i, j: (0, 0)),
                  pl.BlockSpec(pw.shape, lambda i, j: (0, 0, 0)),
                  pl.BlockSpec((1, d), lambda i, j: (0, 0)),
                  pl.BlockSpec((1, d), lambda i, j: (0, 0)),
                  pl.BlockSpec((e, d), lambda i, j: (0, 0))],
        out_specs=out_specs, out_shape=out_shape,
        scratch_shapes=[pltpu.VMEM((tm + 2 * POOL_HALO, d), F32)],
        name="pool_mixer",
        compiler_params=_cparams("arbitrary", "arbitrary"),
    )(x, x, x, mod, g1, pw, ps, g2, rt)


def _topk_kernel(aff_ref, selpos_ref, gsel_ref, st_ref, stv_ref, most_ref, *, cap):
    aff = aff_ref[0]
    e, n = aff.shape
    nt = n // LANES
    capf = float(cap)

    def bit_step(i, tau):
        cand = tau | jnp.left_shift(jnp.int32(1), 30 - i)
        cnt = jnp.sum(jnp.where(aff >= pltpu.bitcast(cand, F32), 1.0, 0.0), axis=1, keepdims=True)
        return jnp.where(cnt >= capf, cand, tau)

    tau = lax.fori_loop(0, 31, bit_step, jnp.zeros((e, 1), I32))
    gt = aff >= pltpu.bitcast(tau + 1, F32)
    eq = (aff >= pltpu.bitcast(tau, F32)) & jnp.logical_not(gt)
    need = capf - jnp.sum(jnp.where(gt, 1.0, 0.0), axis=1, keepdims=True)

    r = lax.broadcasted_iota(I32, (LANES, LANES), 0)
    c = lax.broadcasted_iota(I32, (LANES, LANES), 1)
    upper = jnp.where(r < c, 1.0, 0.0).astype(BF16)

    def excl_prefix(mask_f):
        off = jnp.zeros((e, 1), F32)
        outs, starts = [], []
        for tb in range(nt):
            blk = mask_f[:, tb * LANES:(tb + 1) * LANES]
            outs.append(_dot(blk.astype(BF16), upper) + off)
            starts.append(off)
            off = off + jnp.sum(blk, axis=1, keepdims=True)
        return jnp.concatenate(outs, axis=1), starts

    eq_rank, _ = excl_prefix(jnp.where(eq, 1.0, 0.0))
    sel = gt | (eq & (eq_rank < need))
    pos, starts = excl_prefix(jnp.where(sel, 1.0, 0.0))
    selpos = jnp.where(sel, pos, -1.0)
    selpos_ref[0] = selpos.astype(I32)

    gsel_ref[0] = jnp.where(sel, aff, 0.0)

    lane = lax.broadcasted_iota(I32, (e, nt), 1)
    st = jnp.zeros((e, nt), F32)
    most = jnp.zeros((e, 1), F32)
    for tb in range(nt):
        st = jnp.where(lane == tb, starts[tb], st)
        stv_ref[0, tb] = starts[tb].astype(I32)
        nxt = starts[tb + 1] if tb + 1 < nt else capf
        most = jnp.maximum(most, nxt - starts[tb])
    st_ref[0] = st.astype(I32)
    most_ref[0] = jnp.max(most, axis=0, keepdims=True).astype(I32)


def _topk(aff_t, cap):
    b, e, n = aff_t.shape
    nt = n // LANES
    return pl.pallas_call(
        functools.partial(_topk_kernel, cap=cap),
        grid=(b,),
        in_specs=[pl.BlockSpec((1, e, n), lambda i: (i, 0, 0))],
        out_specs=[pl.BlockSpec((1, e, n), lambda i: (i, 0, 0)),
                   pl.BlockSpec((1, e, n), lambda i: (i, 0, 0)),
                   pl.BlockSpec((1, e, nt), lambda i: (i, 0, 0)),
                   pl.BlockSpec((1, nt, e, 1), lambda i: (i, 0, 0, 0)),
                   pl.BlockSpec((1, 1, 1), lambda i: (i, 0, 0))],
        out_shape=[jax.ShapeDtypeStruct((b, e, n), I32),
                   jax.ShapeDtypeStruct((b, e, n), F32),
                   jax.ShapeDtypeStruct((b, e, nt), I32),
                   jax.ShapeDtypeStruct((b, nt, e, 1), I32),
                   jax.ShapeDtypeStruct((b, 1, 1), I32)],
        name="moe_topk",
        compiler_params=_cparams("arbitrary"),
    )(aff_t)


def _gather_kernel(st_ref, most_ref, selpos_ref, gsel_ref, stv_ref, h_ref, xs_ref, gs_ref, *, cap):
    bi = pl.program_id(0)
    ne = xs_ref.shape[0]
    nt = h_ref.shape[1] // LANES
    first_cols = pl.program_id(1) == 0
    xs_ref[...] = jnp.zeros_like(xs_ref)

    @pl.when(first_cols)
    def _():
        gs_ref[...] = jnp.zeros_like(gs_ref)

    def run(win, unroll, with_gate):
        riota = lax.broadcasted_iota(I32, (ne, win, LANES), 1)

        def body(tb, _):
            rows = pl.ds(pl.multiple_of(tb * LANES, LANES), LANES)
            slot = selpos_ref[0, :, pl.ds(tb, 1), :]
            p0v = jnp.minimum(jnp.bitwise_and(stv_ref[0, tb], -BF16_ROWS), cap - win)
            onehot = jnp.where(slot - p0v[:, :, None] == riota, 1.0, 0.0)
            res = _dot(onehot.reshape(ne * win, LANES).astype(BF16), h_ref[0, rows, :])
            if with_gate:
                gate = jnp.sum(onehot * gsel_ref[0, :, pl.ds(tb, 1), :], axis=-1, keepdims=True)
            for ei in range(ne):
                start = st_ref[(bi * ne + ei) * nt + tb]
                p0 = pl.multiple_of(jnp.minimum(jnp.bitwise_and(start, -BF16_ROWS), cap - win), BF16_ROWS)
                xs_ref[ei, 0, pl.ds(p0, win), :] += res[ei * win:(ei + 1) * win].astype(xs_ref.dtype)
                if with_gate:
                    gs_ref[ei, 0, pl.ds(p0, win), :] += gate[ei]
            return 0

        lax.fori_loop(0, nt, body, 0, unroll=unroll)

    fast = min(GATHER_FAST_WIN, cap)
    full = min(GATHER_WIN, cap)
    small = most_ref[bi] <= fast - BF16_ROWS
    for is_small, win, unroll in ((True, fast, min(4, nt)), (False, full, 1)):
        for with_gate in (True, False):
            @pl.when((small == is_small) & (first_cols == with_gate))
            def _(win=win, unroll=unroll, with_gate=with_gate):
                run(win, unroll, with_gate)


def _gather(st, most, selpos, gsel, stv, h2, cap):
    b, n, d = h2.shape
    e = selpos.shape[1]
    nt = n // LANES
    dh = d // 2
    grid_spec = pltpu.PrefetchScalarGridSpec(
        num_scalar_prefetch=2,
        grid=(b, d // dh),
        in_specs=[pl.BlockSpec((1, e, nt, LANES), lambda i, k, p, q: (i, 0, 0, 0)),
                  pl.BlockSpec((1, e, nt, LANES), lambda i, k, p, q: (i, 0, 0, 0)),
                  pl.BlockSpec((1, nt, e, 1), lambda i, k, p, q: (i, 0, 0, 0)),
                  pl.BlockSpec((1, n, dh), lambda i, k, p, q: (i, 0, k))],
        out_specs=[pl.BlockSpec((e, 1, cap, dh), lambda i, k, p, q: (0, i, 0, k)),
                   pl.BlockSpec((e, 1, cap, 1), lambda i, k, p, q: (0, i, 0, 0))],
    )
    return pl.pallas_call(
        functools.partial(_gather_kernel, cap=cap),
        grid_spec=grid_spec,
        out_shape=[jax.ShapeDtypeStruct((e, b, cap, d), BF16),
                   jax.ShapeDtypeStruct((e, b, cap, 1), F32)],
        name="moe_gather",
        compiler_params=_cparams("arbitrary", "arbitrary"),
    )(st.reshape(-1), most.reshape(-1), selpos.reshape(b, e, nt, LANES), gsel.reshape(b, e, nt, LANES), stv, h2)


def _ffn_kernel(*refs, ngroups, nsteps):
    xs = refs[:ngroups]
    gs = refs[ngroups:2 * ngroups]
    wg_ref, wu_ref, wd_ref = refs[2 * ngroups:2 * ngroups + 3]
    ys = refs[2 * ngroups + 3:3 * ngroups + 3]
    accs = refs[3 * ngroups + 3:]
    f = pl.program_id(1)
    last = nsteps - 1

    def step(first, final):
        wg = wu = wd = None
        for x_ref, g_ref, y_ref, acc_ref in zip(xs, gs, ys, accs):
            rows = x_ref.shape[1]
            tr = min(ROW_TILE, rows)
            for r0 in range(0, rows, tr):
                sl = slice(r0, r0 + tr)
                x = x_ref[0, sl, :]
                wg = wg_ref[0, 0].astype(BF16) if wg is None else wg
                g = _dot(x, wg)
                wu = wu_ref[0, 0].astype(BF16) if wu is None else wu
                u = _dot(x, wu)
                wd = wd_ref[0, 0].astype(BF16) if wd is None else wd
                part = _dot((g * _sigmoid(g) * u).astype(BF16), wd)
                total = part if first else acc_ref[sl, :] + part
                if final:
                    y_ref[0, sl, :] = (total * g_ref[0, sl, :]).astype(y_ref.dtype)
                else:
                    acc_ref[sl, :] = total

    if nsteps == 1:
        step(True, True)
        return
    variants = [(True, False, f == 0), (False, True, f == last)]
    if nsteps > 2:
        variants.append((False, False, (f > 0) & (f < last)))
    for first, final, cond in variants:
        @pl.when(cond)
        def _(first=first, final=final):
            step(first, final)


def _ffn(xs_list, gs_list, wg, wu, wd, layer):
    _, e, d, ff = wg.shape
    tf = min(FF_TILE, ff)
    ng = len(xs_list)
    in_specs = [pl.BlockSpec((1, x.shape[1], d), lambda i, j: (i, 0, 0)) for x in xs_list]
    in_specs += [pl.BlockSpec((1, x.shape[1], 1), lambda i, j: (i, 0, 0)) for x in xs_list]
    in_specs += [pl.BlockSpec((1, 1, d, tf), lambda i, j: (layer, i, 0, j)),
                 pl.BlockSpec((1, 1, d, tf), lambda i, j: (layer, i, 0, j)),
                 pl.BlockSpec((1, 1, tf, d), lambda i, j: (layer, i, j, 0))]
    return pl.pallas_call(
        functools.partial(_ffn_kernel, ngroups=ng, nsteps=ff // tf),
        grid=(e, ff // tf),
        in_specs=in_specs,
        out_specs=[pl.BlockSpec((1, x.shape[1], d), lambda i, j: (i, 0, 0)) for x in xs_list],
        out_shape=[jax.ShapeDtypeStruct(x.shape, BF16) for x in xs_list],
        scratch_shapes=[pltpu.VMEM((x.shape[1], d), F32) for x in xs_list],
        name="moe_ffn",
        compiler_params=_cparams("arbitrary", "arbitrary"),
    )(*xs_list, *gs_list, wg, wu, wd)


def _combine_kernel(st_ref, most_ref, selpos_ref, stv_ref, y_ref, x_ref, mod_ref, fg_ref, o_ref, ystk_ref, *,
                    cap, nt_total, final):
    bi = pl.program_id(0)
    tg = pl.program_id(2)
    ne = y_ref.shape[0]
    tm = x_ref.shape[1]
    mgate = mod_ref[0][5:6]

    def emit(rows, acc):
        res = x_ref[0, rows, :] + mgate * acc
        o_ref[0, rows, :] = _rms(res, fg_ref[...]) if final else res

    fwin = min(SCATTER_FAST_WIN, cap)
    swin = min(SCATTER_WIN, cap)

    def window_start(ei, tb, win):
        start = st_ref[(bi * ne + ei) * nt_total + tb]
        return pl.multiple_of(jnp.minimum(jnp.bitwise_and(start, -BF16_ROWS), cap - win), BF16_ROWS)

    def fast_body(t, _):
        tb = tg * (tm // LANES) + t
        rows = pl.ds(pl.multiple_of(t * LANES, LANES), LANES)
        for ei in range(ne):
            p0 = window_start(ei, tb, fwin)
            ystk_ref[ei * fwin:(ei + 1) * fwin, :] = y_ref[ei, 0, pl.ds(p0, fwin), :]
        slot = selpos_ref[0, :, pl.ds(tb, 1), :]
        p0v = jnp.minimum(jnp.bitwise_and(stv_ref[0, tb], -BF16_ROWS), cap - fwin)
        riota = lax.broadcasted_iota(I32, (ne, fwin, LANES), 1)
        onehot_t = jnp.where(slot - p0v[:, :, None] == riota, 1.0, 0.0)
        acc = _tn(onehot_t.reshape(ne * fwin, LANES).astype(BF16), ystk_ref[...])
        emit(rows, acc)
        return 0

    def full_body(t, _):
        tb = tg * (tm // LANES) + t
        rows = pl.ds(pl.multiple_of(t * LANES, LANES), LANES)
        riota = lax.broadcasted_iota(I32, (swin, LANES), 0)
        acc = jnp.zeros((LANES, x_ref.shape[2]), F32)
        for ei in range(ne):
            p0 = window_start(ei, tb, swin)
            slot = selpos_ref[0, ei, pl.ds(tb, 1), :]
            onehot_t = jnp.where(slot - p0 == riota, 1.0, 0.0).astype(BF16)
            acc = acc + _tn(onehot_t, y_ref[ei, 0, pl.ds(p0, swin), :])
        emit(rows, acc)
        return 0

    most = most_ref[bi]

    @pl.when(most <= fwin - BF16_ROWS)
    def _():
        lax.fori_loop(0, tm // LANES, fast_body, 0)

    @pl.when(most > fwin - BF16_ROWS)
    def _():
        lax.fori_loop(0, tm // LANES, full_body, 0)


def _combine(st, most, selpos, stv, y, x, mod, cap, final_g=None):
    b, n, d = x.shape
    e = y.shape[0]
    tm = min(ROW_TILE, n)
    dh = d
    nt = n // LANES
    final = final_g is not None
    fg = final_g if final else jnp.ones((1, d), F32)
    grid_spec = pltpu.PrefetchScalarGridSpec(
        num_scalar_prefetch=2,
        grid=(b, d // dh, n // tm),
        in_specs=[pl.BlockSpec((1, e, nt, LANES), lambda i, k, j, p, q: (i, 0, 0, 0)),
                  pl.BlockSpec((1, nt, e, 1), lambda i, k, j, p, q: (i, 0, 0, 0)),
                  pl.BlockSpec((e, 1, cap, dh), lambda i, k, j, p, q: (0, i, 0, k)),
                  pl.BlockSpec((1, tm, dh), lambda i, k, j, p, q: (i, j, k)),
                  pl.BlockSpec((1, 6, dh), lambda i, k, j, p, q: (i, 0, k)),
                  pl.BlockSpec((1, dh), lambda i, k, j, p, q: (0, k))],
        out_specs=pl.BlockSpec((1, tm, dh), lambda i, k, j, p, q: (i, j, k)),
        scratch_shapes=[pltpu.VMEM((e * min(SCATTER_FAST_WIN, cap), dh), BF16)],
    )
    return pl.pallas_call(
        functools.partial(_combine_kernel, cap=cap, nt_total=nt, final=final),
        grid_spec=grid_spec,
        out_shape=jax.ShapeDtypeStruct((b, n, d), F32),
        name="moe_combine",
        compiler_params=_cparams("arbitrary", "arbitrary", "arbitrary"),
    )(st.reshape(-1), most.reshape(-1), selpos.reshape(b, e, nt, LANES), stv, y, x, mod, fg)


def _rot_cols(w):
    q = MLA_ROPE // 4
    return jnp.concatenate([-w[:, q:2 * q], w[:, :q], -w[:, 3 * q:], w[:, 2 * q:3 * q]], axis=1)


def _prep_even(w_in, wq_b, wkv_b):
    kpe = w_in[:, -MLA_ROPE:]
    win = jnp.concatenate([w_in, _rot_cols(kpe)], axis=1).astype(BF16)
    qh = wq_b.reshape(MLA_Q_RANK, MLA_HEADS, MLA_NOPE + MLA_ROPE)
    nope = [qh[:, h, :MLA_NOPE] for h in range(MLA_HEADS)]
    pe = [jnp.concatenate([qh[:, h, MLA_NOPE:], _rot_cols(qh[:, h, MLA_NOPE:])], axis=1) for h in range(MLA_HEADS)]
    wq = jnp.concatenate(nope + pe, axis=1).astype(BF16)
    kvh = wkv_b.reshape(MLA_KV_RANK, MLA_HEADS, MLA_NOPE + MLA_V)
    wkv = jnp.concatenate([kvh[:, h, :MLA_NOPE] for h in range(MLA_HEADS)]
                          + [kvh[:, h, MLA_NOPE:] for h in range(MLA_HEADS)], axis=1).astype(BF16)
    return win, wq, wkv


def _rope_tables(n):
    rows = n // GRID_W
    row = jnp.repeat(jnp.arange(rows, dtype=F32), GRID_W)
    col = jnp.tile(jnp.arange(GRID_W, dtype=F32), rows)
    half = MLA_ROPE // 2
    inv = 1.0 / (ROPE_THETA ** (jnp.arange(0, half, 2, dtype=F32) / half))
    ar = row[:, None] * inv[None, :]
    ac = col[:, None] * inv[None, :]
    ang = jnp.concatenate([ar, ar, ac, ac], axis=-1)
    z = jnp.zeros((n, LANES - MLA_ROPE), F32)
    return jnp.concatenate([jnp.cos(ang), z], axis=1), jnp.concatenate([jnp.sin(ang), z], axis=1)


def _moe(streams, wg, wu, wd, layer, final_g=None):
    routed = []
    for xx, mm, h2, aff_t in streams:
        b, n, d = xx.shape
        e = aff_t.shape[1]
        cap = EC_FACTOR * n // e
        selpos, gsel, st, stv, most = _topk(aff_t, cap)
        xs, gs = _gather(st, most, selpos, gsel, stv, h2, cap)
        routed.append((st, most, selpos, stv, cap, xs, gs))
    flat = lambda a: a.reshape(a.shape[0], -1, a.shape[3])
    ys = _ffn([flat(r[5]) for r in routed], [flat(r[6]) for r in routed], wg, wu, wd, layer)
    outs = []
    for si, ((xx, mm, _, _), (st, most, selpos, stv, cap, xs, gs), y) in enumerate(zip(streams, routed, ys)):
        outs.append(_combine(st, most, selpos, stv, y.reshape(xs.shape), xx, mm, cap,
                             final_g if si == 0 else None))
    return outs


def kernel(x, c, ctx, c_ctx, ada_w, ada_b, norm1_g, norm2_g, w_in, hg_lb, hg_norm_g, mla_qn_g, mla_wq_b,
           mla_kvn_g, mla_wkv_b, w_out, pool_w, pool_scale, router_w, exp_wg, exp_wu, exp_wd, final_g):
    b, n, d = x.shape
    nctx = ctx.shape[1]
    depth = ada_w.shape[0]
    n_even = w_in.shape[0]
    assert b + 1 <= SUBLANES and n % LANES == 0 and nctx % LANES == 0 and n % GRID_W == 0

    cc = jnp.zeros((SUBLANES, d), F32).at[:b].set(c).at[b].set(c_ctx)
    mods = _ada(cc, ada_w, ada_b)
    mod_lat = [mods[l, :b].reshape(b, 6, d) for l in range(depth)]
    mod_ctx = [jnp.broadcast_to(mods[l, b].reshape(1, 6, d), (b, 6, d)) for l in range(depth)]

    lb_all = jnp.cumsum(jax.nn.softmax(hg_lb.astype(F32), axis=0), axis=0)
    lb_all = lb_all - lb_all[:1]
    cos1, sin1 = _rope_tables(n)
    cos_c = jnp.concatenate([jnp.ones((nctx, MLA_ROPE), F32), jnp.zeros((nctx, LANES - MLA_ROPE), F32)], axis=1)
    sin_c = jnp.zeros((nctx, LANES), F32)

    last_reader = 2 * (n_even - 1)
    row = lambda v: v.reshape(1, -1)
    x_lat, x_ctx = x, ctx
    for l in range(depth):
        j = l // 2
        ctx_in = l <= last_reader
        ctx_out = l < last_reader
        g2, rt = row(norm2_g[l]), router_w[l].T
        if l % 2 == 0:
            win, wq, wkv = _prep_even(w_in[j], mla_wq_b[j], mla_wkv_b[j])
            lb = lb_all[j]
            lbp = jnp.stack([lb[0], 1.0 - lb[0], lb[1], 1.0 - lb[1]])
            common = (row(norm1_g[l]), win, row(mla_qn_g[j]), wq, row(mla_kvn_g[j]), wkv)
            hg_l, q_l, k_l, v_l = _in_proj(x_lat, mod_lat[l], *common, cos1, sin1)
            hg_c = None
            ks, vs = [k_l], [v_l]
            if ctx_in:
                hg_c, q_c, k_c, v_c = _in_proj(x_ctx, mod_ctx[l], *common, cos_c, sin_c)
                ks, vs = [k_c, k_l], [v_c, v_l]
            hgo_l, hgo_c = _hgrn(hg_l, hg_c, lbp, row(hg_norm_g[j]), ctx_out)
            mla_l = _attention(q_l, ks, vs)
            wo = w_out[j].astype(BF16)
            streams = [(mod_lat[l],) + tuple(_out_proj(hgo_l, mla_l, x_lat, mod_lat[l], wo, g2, rt))]
            if ctx_out:
                mla_c = _attention(q_c, [k_c], [v_c])
                streams.append((mod_ctx[l],) + tuple(_out_proj(hgo_c, mla_c, x_ctx, mod_ctx[l], wo, g2, rt)))
        else:
            pw = pool_w[j].astype(BF16)
            pool_args = (row(norm1_g[l]), pw, row(pool_scale[j]), g2, rt)
            streams = [(mod_lat[l],) + tuple(_pool(x_lat, mod_lat[l], *pool_args))]
            if ctx_out:
                streams.append((mod_ctx[l],) + tuple(_pool(x_ctx, mod_ctx[l], *pool_args)))
        outs = _moe([(xn, mm, h2, aff) for mm, xn, h2, aff in streams], exp_wg, exp_wu, exp_wd, l,
                    final_g=row(final_g) if l == depth - 1 else None)
        x_lat = outs[0]
        if ctx_out:
            x_ctx = outs[1]
    return x_lat
```
